```python
import jax
import jax.numpy as jnp
from jax import lax
import numpy as np

D_MODEL = 2048
BATCH = 4
SEQ = 4096
DEPTH = 4

GRID_W = 64
CTX_LEN = 256
N_MIXERS = 3
N_MOD = 9
D_FF = 5632
EPS = 1e-6
GLA_HEADS = 4
GLA_DK = D_MODEL // (2 * GLA_HEADS)
GLA_DV = D_MODEL // GLA_HEADS
GLA_RANK = 16
GLA_TAU = 16.0
GLA_CHUNK = 64
GLA_HK = GLA_HEADS * GLA_DK
GLA_HV = GLA_HEADS * GLA_DV
GLA_SPLITS = (GLA_HK, 2 * GLA_HK, 2 * GLA_HK + GLA_HV, 2 * GLA_HK + 2 * GLA_HV,
              2 * GLA_HK + 2 * GLA_HV + GLA_RANK)
GLA_IN = 2 * GLA_HK + 2 * GLA_HV + 2 * GLA_RANK
FNET_GROUPS = 8
FNET_GW = D_MODEL // FNET_GROUPS
CONV_W = 31

kernel_name = "hybrid_gla_fnet_conformer_dit"


def rmsnorm(x, g):
    xf = x.astype(jnp.float32)
    y = xf * lax.rsqrt(jnp.mean(xf * xf, axis=-1, keepdims=True) + EPS)
    return (y * g).astype(x.dtype)


def layernorm(x, g, b):
    xf = x.astype(jnp.float32)
    mu = jnp.mean(xf, axis=-1, keepdims=True)
    var = jnp.mean(jnp.square(xf - mu), axis=-1, keepdims=True)
    return ((xf - mu) * lax.rsqrt(var + EPS) * g + b).astype(x.dtype)


def adaln(cond, w_mod, b_mod):
    m = jax.nn.silu(cond) @ w_mod + b_mod
    return m.reshape(m.shape[:-1] + (N_MOD, D_MODEL))


def pre_norm(x, mod, k, g):
    return rmsnorm(x, g) * (1.0 + mod[..., 3 * k + 1, :]) + mod[..., 3 * k, :]


def ffn_half(x, mod, k, g, w_gate, w_up, w_down):
    h = pre_norm(x, mod, k, g)
    y = (jax.nn.silu(h @ w_gate) * (h @ w_up)) @ w_down
    return x + 0.5 * mod[..., 3 * k + 2, :] * y


def _flip(t):
    return jnp.flip(t, axis=1)


def gla_project(h, w_in, w_gate_up, b_gate):
    bq, L, _ = h.shape
    p = (h @ w_in).astype(jnp.float32)
    q, k, v, r, z_f, z_b = jnp.split(p, GLA_SPLITS, axis=-1)
    q = q.reshape(bq, L, GLA_HEADS, GLA_DK) * (GLA_DK ** -0.5)
    k = k.reshape(bq, L, GLA_HEADS, GLA_DK)
    v = v.reshape(bq, L, GLA_HEADS, GLA_DV)
    g_f = (jax.nn.log_sigmoid(z_f @ w_gate_up[0] + b_gate[0]) / GLA_TAU).reshape(bq, L, GLA_HEADS, GLA_DK)
    g_b = (jax.nn.log_sigmoid(z_b @ w_gate_up[1] + b_gate[1]) / GLA_TAU).reshape(bq, L, GLA_HEADS, GLA_DK)
    return q, k, v, r, g_f, g_b


def gla_final_state(k, v, g):
    G = jnp.cumsum(g, axis=1)
    kd = k * jnp.exp(G[:, -1:] - G)
    return jnp.einsum("blhd,blhv->bhdv", kd, v)


def gla_chunked(q, k, v, g, s0):
    bq, L, H, dk = q.shape
    n = L // GLA_CHUNK

    def blocks(t):
        return t.reshape(bq, n, GLA_CHUNK, H, t.shape[-1]).transpose(1, 0, 3, 2, 4)

    qb, kb, vb, gb = blocks(q), blocks(k), blocks(v), blocks(g)
    bcum = jnp.cumsum(gb, axis=3)
    b_last = bcum[:, :, :, -1:, :]
    ref = bcum[:, :, :, GLA_CHUNK // 2 - 1:GLA_CHUNK // 2, :]
    q_intra = qb * jnp.exp(bcum - ref)
    k_intra = kb * jnp.exp(ref - bcum)
    mask = jnp.tril(jnp.ones((GLA_CHUNK, GLA_CHUNK), dtype=bool))
    att = jnp.where(mask, jnp.einsum("nbhtd,nbhsd->nbhts", q_intra, k_intra), 0.0)
    o_intra = jnp.einsum("nbhts,nbhsv->nbhtv", att, vb)
    q_inter = qb * jnp.exp(bcum)
    k_state = kb * jnp.exp(b_last - bcum)
    decay = jnp.exp(b_last[:, :, :, 0, :])

    def step(s, xs):
        q_n, k_n, v_n, d_n = xs
        o = jnp.einsum("bhtd,bhdv->bhtv", q_n, s)
        s = d_n[..., None] * s + jnp.einsum("bhsd,bhsv->bhdv", k_n, v_n)
        return s, o

    _, o_inter = lax.scan(step, s0, (q_inter, k_state, vb, decay))
    o = o_intra + o_inter
    return o.transpose(1, 0, 3, 2, 4).reshape(bq, L, H, v.shape[-1])


def gla_output(o, r, g_head, w_out, dtype):
    bq, L = o.shape[:2]
    o = o * lax.rsqrt(jnp.mean(o * o, axis=-1, keepdims=True) + EPS) * g_head
    o = o * jax.nn.silu(r.reshape(bq, L, GLA_HEADS, GLA_DV))
    return o.reshape(bq, L, GLA_HV).astype(dtype) @ w_out


def gla_mixer(h_l, h_c, w_in, w_gate_up, b_gate, g_head, w_out, emit_ctx):
    ql, kl, vl, rl, gfl, gbl = gla_project(h_l, w_in, w_gate_up, b_gate)
    qc, kc, vc, rc, gfc, gbc = gla_project(h_c, w_in, w_gate_up, b_gate)
    s_f = gla_final_state(kc, vc, gfc)
    s_b = gla_final_state(_flip(kc), _flip(vc), _flip(gbc))
    o_l = gla_chunked(ql, kl, vl, gfl, s_f) + _flip(
        gla_chunked(_flip(ql), _flip(kl), _flip(vl), _flip(gbl), s_b))
    y_l = gla_output(o_l, rl, g_head, w_out, h_l.dtype)
    y_c = None
    if emit_ctx:
        s0 = jnp.zeros_like(s_f)
        o_c = gla_chunked(qc, kc, vc, gfc, s0) + _flip(
            gla_chunked(_flip(qc), _flip(kc), _flip(vc), _flip(gbc), s0))
        y_c = gla_output(o_c, rc, g_head, w_out, h_c.dtype)
    return y_l, y_c


def fourier_mix(h, w_out, b_out):
    bq, L, _ = h.shape
    hg = h.astype(jnp.float32).reshape(bq, L, FNET_GROUPS, FNET_GW)
    f = jnp.fft.fftn(hg, axes=(1, 3), norm="ortho").real
    return f.reshape(bq, L, D_MODEL).astype(h.dtype) @ w_out + b_out


def conv_module(h, w_pw1, b_pw1, w_dw, b_dw, ln_g, ln_b, w_pw2, b_pw2, n_seg):
    u = h @ w_pw1 + b_pw1
    a, gate = jnp.split(u, 2, axis=-1)
    u = a * jax.nn.sigmoid(gate)
    bq, L, dm = u.shape
    seg = u.reshape(bq * n_seg, L // n_seg, dm)
    y = lax.conv_general_dilated(
        seg, w_dw[:, None, :].astype(seg.dtype), window_strides=(1,),
        padding=((CONV_W // 2, CONV_W // 2),),
        dimension_numbers=("NWC", "WIO", "NWC"), feature_group_count=dm)
    y = y.reshape(bq, L, dm) + b_dw
    y = jax.nn.silu(layernorm(y, ln_g, ln_b))
    return y @ w_pw2 + b_pw2


def setup_inputs(seed: int = 0) -> dict:
    key = jax.random.key(seed)
    ks = iter(jax.random.split(key, 32))
    D = D_MODEL
    n_a = len(range(0, DEPTH, N_MIXERS))
    n_b = len(range(1, DEPTH, N_MIXERS))
    n_c = len(range(2, DEPTH, N_MIXERS))

    def nrm(shape, scale):
        return jax.random.normal(next(ks), shape, jnp.float32) * scale

    return {
        "x": nrm((BATCH, SEQ, D), 1.0),
        "c": nrm((BATCH, D), 1.0),
        "ctx": nrm((BATCH, CTX_LEN, D), 1.0),
        "c_ctx": nrm((D,), 1.0),
        "w_mod": nrm((DEPTH, D, N_MOD * D), 0.5 * D ** -0.5),
        "b_mod": nrm((DEPTH, N_MOD * D), 0.02),
        "norm_g": 1.0 + nrm((DEPTH, 3, D), 0.02),
        "ffn_w_gate": nrm((DEPTH, 2, D, D_FF), D ** -0.5),
        "ffn_w_up": nrm((DEPTH, 2, D, D_FF), D ** -0.5),
        "ffn_w_down": nrm((DEPTH, 2, D_FF, D), D_FF ** -0.5),
        "gla_w_in": nrm((n_a, D, GLA_IN), D ** -0.5),
        "gla_w_gate_up": nrm((n_a, 2, GLA_RANK, GLA_HK), GLA_RANK ** -0.5),
        "gla_b_gate": nrm((n_a, 2, GLA_HK), 0.1),
        "gla_g_head": 1.0 + nrm((n_a, GLA_DV), 0.02),
        "gla_w_out": nrm((n_a, GLA_HV, D), GLA_HV ** -0.5),
        "fnet_w_out": nrm((n_b, D, D), D ** -0.5),
        "fnet_b_out": nrm((n_b, D), 0.02),
        "cm_w_pw1": nrm((n_c, D, 2 * D), D ** -0.5),
        "cm_b_pw1": nrm((n_c, 2 * D), 0.02),
        "cm_w_dw": nrm((n_c, CONV_W, D), CONV_W ** -0.5),
        "cm_b_dw": nrm((n_c, D), 0.02),
        "cm_ln_g": 1.0 + nrm((n_c, D), 0.02),
        "cm_ln_b": nrm((n_c, D), 0.02),
        "cm_w_pw2": nrm((n_c, D, D), D ** -0.5),
        "cm_b_pw2": nrm((n_c, D), 0.02),
        "final_g": 1.0 + nrm((D,), 0.02),
    }


def reference(x, c, ctx, c_ctx, w_mod, b_mod, norm_g, ffn_w_gate, ffn_w_up, ffn_w_down,
              gla_w_in, gla_w_gate_up, gla_b_gate, gla_g_head, gla_w_out,
              fnet_w_out, fnet_b_out,
              cm_w_pw1, cm_b_pw1, cm_w_dw, cm_b_dw, cm_ln_g, cm_ln_b, cm_w_pw2, cm_b_pw2,
              final_g):
    rows = x.shape[1] // GRID_W
    xc = ctx
    for i in range(DEPTH):
        kind = i % N_MIXERS
        j = i // N_MIXERS
        last = i == DEPTH - 1
        ctx_in = (not last) or kind == 0
        mod_l = adaln(c, w_mod[i], b_mod[i])[:, None]
        x = ffn_half(x, mod_l, 0, norm_g[i, 0], ffn_w_gate[i, 0], ffn_w_up[i, 0], ffn_w_down[i, 0])
        h_l = pre_norm(x, mod_l, 1, norm_g[i, 1])
        if ctx_in:
            mod_c = adaln(c_ctx, w_mod[i], b_mod[i])[None, None]
            xc = ffn_half(xc, mod_c, 0, norm_g[i, 0], ffn_w_gate[i, 0], ffn_w_up[i, 0], ffn_w_down[i, 0])
            h_c = pre_norm(xc, mod_c, 1, norm_g[i, 1])
        if kind == 0:
            y_l, y_c = gla_mixer(h_l, h_c, gla_w_in[j], gla_w_gate_up[j], gla_b_gate[j],
                                 gla_g_head[j], gla_w_out[j], emit_ctx=not last)
        elif kind == 1:
            y_l = fourier_mix(h_l, fnet_w_out[j], fnet_b_out[j])
            y_c = fourier_mix(h_c, fnet_w_out[j], fnet_b_out[j]) if not last else None
        else:
            cm = (cm_w_pw1[j], cm_b_pw1[j], cm_w_dw[j], cm_b_dw[j], cm_ln_g[j], cm_ln_b[j],
                  cm_w_pw2[j], cm_b_pw2[j])
            y_l = conv_module(h_l, *cm, rows)
            y_c = conv_module(h_c, *cm, 1) if not last else None
        x = x + mod_l[..., 5, :] * y_l
        x = ffn_half(x, mod_l, 2, norm_g[i, 2], ffn_w_gate[i, 1], ffn_w_up[i, 1], ffn_w_down[i, 1])
        if not last:
            xc = xc + mod_c[..., 5, :] * y_c
            xc = ffn_half(xc, mod_c, 2, norm_g[i, 2], ffn_w_gate[i, 1], ffn_w_up[i, 1], ffn_w_down[i, 1])
    return rmsnorm(x, final_g)
```

```python
import functools
import math

import jax
import jax.numpy as jnp
from jax import lax
from jax.experimental import pallas as pl
from jax.experimental.pallas import tpu as pltpu

F32 = jnp.float32
BF16 = jnp.bfloat16

D = 2048
B = 4
L = 4096
LC = 256
DEPTH = 4
N_MIXERS = 3
N_MOD = 9
DFF = 5632
EPS = 1e-6
GRID_W = 64

TL = B * L
TC = B * LC
T = TL + TC

H = 4
DK = D // (2 * H)
DV = D // H
RANK = 16
TAU = 16.0
CHUNK = 64
HK = H * DK
HV = H * DV
GLA_MAIN = 2 * HK + 2 * HV
GLA_BLK = 256
ZPAD = 128

FG = 8
GW = D // FG
CONV_W = 31
CONV_PAD = 16
CONV_BLK = 256

MOD_ROWS = 8
TM = 1024
TF = 512
VMEM_LIMIT = 60 * 1024 * 1024

_ARB2 = ("arbitrary", "arbitrary")


def _params(sem, limit=VMEM_LIMIT):
    return pltpu.CompilerParams(dimension_semantics=sem, vmem_limit_bytes=limit)


def _mod_row(t, tm):
    return jnp.minimum(t // (L // tm), B)


def _mod_spec(slot, tm):
    return pl.BlockSpec((None, 1, D), lambda i, j: (_mod_row(i, tm) * N_MOD + slot, 0, 0))


def _row_spec(n):
    return pl.BlockSpec((1, n), lambda i, j: (0, 0))


def _rms(x):
    return x * lax.rsqrt(jnp.mean(x * x, axis=-1, keepdims=True) + EPS)


def _prenorm(x, g, scale, shift):
    return (_rms(x) * g) * (1.0 + scale) + shift


def _sigmoid(x):
    return 1.0 / (1.0 + jnp.exp(-x))


def _silu(x):
    return x * _sigmoid(x)


def _log_sigmoid(x):
    return jnp.minimum(x, 0.0) - jnp.log1p(jnp.exp(-jnp.abs(x)))


def _adaln_kernel(c_ref, w_ref, b_ref, o_ref):
    s = _silu(c_ref[...]).astype(BF16)
    o_ref[...] = jnp.dot(s, w_ref[...].astype(BF16), preferred_element_type=F32) + b_ref[...]


def adaln_all(cvec, w_mod, b_mod):
    tn = 1024
    n = N_MOD * D
    return pl.pallas_call(
        _adaln_kernel,
        grid=(DEPTH, n // tn),
        in_specs=[
            pl.BlockSpec((MOD_ROWS, D), lambda i, j: (0, 0)),
            pl.BlockSpec((None, D, tn), lambda i, j: (i, 0, j)),
            pl.BlockSpec((None, 1, tn), lambda i, j: (i, 0, j)),
        ],
        out_specs=pl.BlockSpec((None, MOD_ROWS, tn), lambda i, j: (i, 0, j)),
        out_shape=jax.ShapeDtypeStruct((DEPTH, MOD_ROWS, n), F32),
        compiler_params=_params(_ARB2),
        name="adaln",
    )(cvec, w_mod, b_mod.reshape(DEPTH, 1, n))


def _ffn_kernel(x_ref, sh_ref, sc_ref, gt_ref, g_ref, wg_ref, wu_ref, wd_ref, *rest, n_j, final):
    if final:
        fg_ref, o_ref, h_ref = rest
    else:
        o_ref, h_ref = rest
    j = pl.program_id(1)

    @pl.when(j == 0)
    def _():
        h_ref[...] = _prenorm(x_ref[...], g_ref[...], sc_ref[...], sh_ref[...]).astype(BF16)
        o_ref[...] = jnp.zeros_like(o_ref)

    h = h_ref[...]
    g = jnp.dot(h, wg_ref[...], preferred_element_type=F32)
    u = jnp.dot(h, wu_ref[...], preferred_element_type=F32)
    a = (_silu(g) * u).astype(BF16)
    o_ref[...] += jnp.dot(a, wd_ref[...], preferred_element_type=F32)

    @pl.when(j == n_j - 1)
    def _():
        out = x_ref[...] + (0.5 * gt_ref[...]) * o_ref[...]
        if final:
            out = _rms(out) * fg_ref[...]
        o_ref[...] = out


def ffn_half(x, mods, k, g_row, wg, wu, wd, layer, half, n_tiles, final_g=None):
    n_j = DFF // TF
    final = final_g is not None
    in_specs = [
        pl.BlockSpec((TM, D), lambda i, j: (i, 0), pipeline_mode=pl.Buffered(1)),
        _mod_spec(3 * k, TM), _mod_spec(3 * k + 1, TM), _mod_spec(3 * k + 2, TM),
        _row_spec(D),
        pl.BlockSpec((None, None, D, TF), lambda i, j: (layer, half, 0, j)),
        pl.BlockSpec((None, None, D, TF), lambda i, j: (layer, half, 0, j)),
        pl.BlockSpec((None, None, TF, D), lambda i, j: (layer, half, j, 0)),
    ]
    args = [x, mods, mods, mods, g_row, wg, wu, wd]
    if final:
        in_specs.append(_row_spec(D))
        args.append(final_g)
    return pl.pallas_call(
        functools.partial(_ffn_kernel, n_j=n_j, final=final),
        grid=(n_tiles, n_j),
        in_specs=in_specs,
        out_specs=pl.BlockSpec((TM, D), lambda i, j: (i, 0)),
        out_shape=jax.ShapeDtypeStruct((n_tiles * TM, D), F32),
        scratch_shapes=[pltpu.VMEM((TM, D), BF16)],
        compiler_params=_params(_ARB2),
        name="ffn_half",
    )(*args)


def _mm_kernel(*refs, n_a, n_w, n_e, prologue, epilogue):
    a_refs = refs[:n_a]
    w_refs = refs[n_a:n_a + n_w]
    e_refs = refs[n_a + n_w:n_a + n_w + n_e]
    if prologue is None:
        o_ref = refs[-1]
        h = a_refs[0][...]
    else:
        o_ref, h_ref = refs[-2:]

        @pl.when(pl.program_id(1) == 0)
        def _():
            h_ref[...] = prologue(*a_refs).astype(BF16)

        h = h_ref[...]
    accs = [jnp.dot(h, w[...], preferred_element_type=F32) for w in w_refs]
    o_ref[...] = epilogue(accs, *e_refs).astype(o_ref.dtype)


def fused_matmul(a_ops, a_specs, prologue, k_dim, w_ops, w_specs, e_ops, e_specs, epilogue,
                 n_out, out_dtype, tm, tn, n_rows, name):
    n_j = n_out // tn
    scratch = [] if prologue is None else [pltpu.VMEM((tm, k_dim), BF16)]
    return pl.pallas_call(
        functools.partial(_mm_kernel, n_a=len(a_ops), n_w=len(w_ops), n_e=len(e_ops),
                          prologue=prologue, epilogue=epilogue),
        grid=(n_rows // tm, n_j),
        in_specs=list(a_specs) + list(w_specs) + list(e_specs),
        out_specs=pl.BlockSpec((tm, tn), lambda i, j: (i, j)),
        out_shape=jax.ShapeDtypeStruct((n_rows, n_out), out_dtype),
        scratch_shapes=scratch,
        compiler_params=_params(_ARB2),
        name=name,
    )(*a_ops, *w_ops, *e_ops)


def _prenorm_ops(x, mods, k, g_row, tm):
    ops = [x, mods, mods, g_row]
    specs = [pl.BlockSpec((tm, D), lambda i, j: (i, 0)),
             _mod_spec(3 * k, tm), _mod_spec(3 * k + 1, tm), _row_spec(D)]

    def prologue(x_ref, sh_ref, sc_ref, g_ref):
        return _prenorm(x_ref[...], g_ref[...], sc_ref[...], sh_ref[...])

    return ops, specs, prologue


def _residual_ops(x, mods, slot, tm, tn, bias=None):
    ops = [x, mods]
    specs = [pl.BlockSpec((tm, tn), lambda i, j: (i, j)),
             pl.BlockSpec((None, 1, tn), lambda i, j: (_mod_row(i, tm) * N_MOD + slot, 0, j))]
    if bias is None:
        def epilogue(accs, x_ref, gt_ref):
            return x_ref[...] + gt_ref[...] * accs[0]
    else:
        ops.append(bias)
        specs.append(pl.BlockSpec((1, tn), lambda i, j: (0, j)))

        def epilogue(accs, x_ref, gt_ref, b_ref):
            return x_ref[...] + gt_ref[...] * (accs[0] + b_ref[...])
    return ops, specs, epilogue


def _gla_dir(q_ref, k_ref, v_ref, z_ref, wup_ref, bg_ref, s_ref, o_ref, reverse):
    n = GLA_BLK
    x = jnp.dot(z_ref[...], wup_ref[...], preferred_element_type=F32,
                precision=lax.Precision.HIGHEST) + bg_ref[...]
    g = _log_sigmoid(x) * (1.0 / TAU)
    r = lax.broadcasted_iota(jnp.int32, (n, n), 0)
    c = lax.broadcasted_iota(jnp.int32, (n, n), 1)
    shift = int(math.log2(CHUNK))
    same = jnp.right_shift(r, shift) == jnp.right_shift(c, shift)
    tri = jnp.where(same & ((c >= r) if reverse else (c <= r)), 1.0, 0.0).astype(F32)
    bcum = jnp.dot(tri, g, preferred_element_type=F32, precision=lax.Precision.HIGHEST)
    rr = lax.broadcasted_iota(jnp.int32, (CHUNK, CHUNK), 0)
    cc = lax.broadcasted_iota(jnp.int32, (CHUNK, CHUNK), 1)
    causal = (cc >= rr) if reverse else (cc <= rr)
    order = range(n // CHUNK - 1, -1, -1) if reverse else range(n // CHUNK)
    i_ref = CHUNK // 2 if reverse else CHUNK // 2 - 1
    i_last = 0 if reverse else CHUNK - 1
    for ci in order:
        rows = slice(ci * CHUNK, (ci + 1) * CHUNK)
        bc = bcum[rows]
        ref = bc[i_ref:i_ref + 1]
        b_last = bc[i_last:i_last + 1]
        qc = q_ref[rows, :] * (DK ** -0.5)
        kc = k_ref[rows, :]
        vc = v_ref[rows, :].astype(BF16)
        q_in = (qc * jnp.exp(bc - ref)).astype(BF16)
        k_in = (kc * jnp.exp(ref - bc)).astype(BF16)
        att = lax.dot_general(q_in, k_in, (((1,), (1,)), ((), ())), preferred_element_type=F32)
        att = jnp.where(causal, att, 0.0).astype(BF16)
        o = jnp.dot(att, vc, preferred_element_type=F32)
        q_it = (qc * jnp.exp(bc)).astype(BF16)
        st_old = s_ref[...]
        o = o + lax.dot_general(q_it, st_old.astype(BF16), (((1,), (1,)), ((), ())),
                                preferred_element_type=F32)
        o_ref[rows, :] = o
        k_st = (kc * jnp.exp(b_last - bc)).astype(BF16)
        s_ref[...] = jnp.exp(b_last) * st_old + lax.dot_general(
            vc, k_st, (((0,), (0,)), ((), ())), preferred_element_type=F32)


def _gla_kernel(qf, kf, vf, zf, qb, kb, vb, zb, wf, wb, bf, bb, of_ref, ob_ref, sf_ref, sb_ref):
    @pl.when(pl.program_id(2) == 0)
    def _():
        sf_ref[...] = jnp.zeros_like(sf_ref)
        sb_ref[...] = jnp.zeros_like(sb_ref)

    _gla_dir(qf, kf, vf, zf, wf, bf, sf_ref, of_ref, reverse=False)
    _gla_dir(qb, kb, vb, zb, wb, bb, sb_ref, ob_ref, reverse=True)


def gla_scan(p, z, wup_pad, b_gate):
    nl = L // GLA_BLK
    ctx0 = TL // GLA_BLK

    def row_f(b, s):
        return jnp.where(s == 0, ctx0 + b, b * nl + s - 1)

    def row_b(b, s):
        return jnp.where(s == 0, ctx0 + b, b * nl + nl - s)

    def specs(row):
        return [
            pl.BlockSpec((GLA_BLK, DK), lambda b, h, s: (row(b, s), h)),
            pl.BlockSpec((GLA_BLK, DK), lambda b, h, s: (row(b, s), H + h)),
            pl.BlockSpec((GLA_BLK, DV), lambda b, h, s: (row(b, s), (2 * HK) // DV + h)),
            pl.BlockSpec((GLA_BLK, ZPAD), lambda b, h, s: (row(b, s), 0)),
        ]

    in_specs = specs(row_f) + specs(row_b) + [
        pl.BlockSpec((None, ZPAD, DK), lambda b, h, s: (0, 0, h)),
        pl.BlockSpec((None, ZPAD, DK), lambda b, h, s: (1, 0, h)),
        pl.BlockSpec((None, 1, DK), lambda b, h, s: (0, 0, h)),
        pl.BlockSpec((None, 1, DK), lambda b, h, s: (1, 0, h)),
    ]
    out_sds = jax.ShapeDtypeStruct((T, HV), F32)
    return pl.pallas_call(
        _gla_kernel,
        grid=(B, H, nl + 1),
        in_specs=in_specs,
        out_specs=[pl.BlockSpec((GLA_BLK, DV), lambda b, h, s: (row_f(b, s), h)),
                   pl.BlockSpec((GLA_BLK, DV), lambda b, h, s: (row_b(b, s), h))],
        out_shape=[out_sds, out_sds],
        scratch_shapes=[pltpu.VMEM((DV, DK), F32), pltpu.VMEM((DV, DK), F32)],
        compiler_params=_params(("arbitrary",) * 3),
        name="gla_scan",
    )(p, p, p, z, p, p, p, z, wup_pad, wup_pad, b_gate, b_gate)


def gla_layer(x, mods, g_row, w_in, w_gate_up, b_gate, g_head, w_out, n_out_rows):
    a_ops, a_specs, prologue = _prenorm_ops(x, mods, 1, g_row, TM)
    w_main = w_in[:, :GLA_MAIN].astype(BF16)
    w_z = jnp.pad(w_in[:, GLA_MAIN:], ((0, 0), (0, ZPAD - 2 * RANK))).astype(BF16)
    plain = lambda accs: accs[0]
    tn = 512
    p = fused_matmul(a_ops, a_specs, prologue, D, [w_main],
                     [pl.BlockSpec((D, tn), lambda i, j: (0, j))], [], [], plain,
                     GLA_MAIN, F32, TM, tn, T, "gla_in")
    z = fused_matmul(a_ops, a_specs, prologue, D, [w_z],
                     [pl.BlockSpec((D, ZPAD), lambda i, j: (0, j))], [], [], plain,
                     ZPAD, F32, TM, ZPAD, T, "gla_rank")
    wup_pad = jnp.zeros((2, ZPAD, HK), F32)
    wup_pad = wup_pad.at[0, :RANK].set(w_gate_up[0]).at[1, RANK:2 * RANK].set(w_gate_up[1])
    o_f, o_b = gla_scan(p, z, wup_pad, b_gate.reshape(2, 1, HK))

    def out_prologue(of_ref, ob_ref, r_ref, gh_ref):
        parts = []
        for h in range(H):
            cols = slice(h * DV, (h + 1) * DV)
            o = of_ref[:, cols] + ob_ref[:, cols]
            parts.append(_rms(o) * gh_ref[...] * _silu(r_ref[:, cols]))
        return jnp.concatenate(parts, axis=-1)

    tmo = 512
    o_ops = [o_f, o_b, p, g_head.reshape(1, DV)]
    o_specs = [pl.BlockSpec((tmo, HV), lambda i, j: (i, 0)),
               pl.BlockSpec((tmo, HV), lambda i, j: (i, 0)),
               pl.BlockSpec((tmo, HV), lambda i, j: (i, (2 * HK + HV) // HV)),
               _row_spec(DV)]
    e_ops, e_specs, epilogue = _residual_ops(x, mods, 5, tmo, tn)
    return fused_matmul(o_ops, o_specs, out_prologue, HV, [w_out.astype(BF16)],
                        [pl.BlockSpec((HV, tn), lambda i, j: (0, j))], e_ops, e_specs, epilogue,
                        D, F32, tmo, tn, n_out_rows, "gla_out")


def _fnet_group_kernel(x_ref, sh_ref, sc_ref, g_ref, cs_ref, o_ref):
    h = _prenorm(x_ref[...], g_ref[...], sc_ref[...], sh_ref[...]).astype(BF16)
    cs = cs_ref[...]
    for g in range(FG):
        ab = jnp.dot(h[:, g * GW:(g + 1) * GW], cs, preferred_element_type=F32).astype(BF16)
        o_ref[:, g * GW:(g + 1) * GW] = ab[:, :GW]
        o_ref[:, D + g * GW:D + (g + 1) * GW] = ab[:, GW:]


def _fnet_pos_kernel(c_ref, s_ref, a_ref, b_ref, *rest, scale):
    o_ref = rest[-1]
    acc = jnp.dot(c_ref[...], a_ref[...], preferred_element_type=F32)
    acc = acc - jnp.dot(s_ref[...], b_ref[...], preferred_element_type=F32)
    o_ref[...] = (acc * scale).astype(o_ref.dtype)


def _dft_tables(n):
    idx = jnp.arange(n, dtype=jnp.int32)
    ang = ((idx[:, None] * idx[None, :]) % n).astype(F32) * (2.0 * math.pi / n)
    return jnp.cos(ang), jnp.sin(ang)


def _fnet_pos(ab, f_prev, seq, row0, tm, tn):
    cos_t, sin_t = _dft_tables(seq)
    cos_t, sin_t = cos_t.astype(BF16), sin_t.astype(BF16)
    blk0 = row0 // seq
    n_m = seq // tm
    n_n = D // tn
    in_specs = [
        pl.BlockSpec((tm, seq), lambda b, n, m: (m, 0)),
        pl.BlockSpec((tm, seq), lambda b, n, m: (m, 0)),
        pl.BlockSpec((seq, tn), lambda b, n, m: (blk0 + b, n)),
        pl.BlockSpec((seq, tn), lambda b, n, m: (blk0 + b, n_n + n)),
    ]
    args = [cos_t, sin_t, ab, ab]
    aliases = {}
    if f_prev is not None:
        in_specs.append(pl.BlockSpec(memory_space=pl.ANY))
        args.append(f_prev)
        aliases = {4: 0}
    return pl.pallas_call(
        functools.partial(_fnet_pos_kernel, scale=1.0 / math.sqrt(seq * GW)),
        grid=(B, n_n, n_m),
        in_specs=in_specs,
        out_specs=pl.BlockSpec((tm, tn), lambda b, n, m: ((blk0 + b) * n_m + m, n)),
        out_shape=jax.ShapeDtypeStruct((T, D), BF16),
        input_output_aliases=aliases,
        compiler_params=_params(("arbitrary",) * 3),
        name="fnet_pos",
    )(*args)


def fnet_layer(x, mods, g_row, w_out, b_out):
    cos_g, sin_g = _dft_tables(GW)
    cs = jnp.concatenate([cos_g, sin_g], axis=1).astype(BF16)
    tmg = 512
    ab = pl.pallas_call(
        _fnet_group_kernel,
        grid=(T // tmg, 1),
        in_specs=[pl.BlockSpec((tmg, D), lambda i, j: (i, 0)),
                  _mod_spec(3, tmg), _mod_spec(4, tmg), _row_spec(D),
                  pl.BlockSpec((GW, 2 * GW), lambda i, j: (0, 0))],
        out_specs=pl.BlockSpec((tmg, 2 * D), lambda i, j: (i, 0)),
        out_shape=jax.ShapeDtypeStruct((T, 2 * D), BF16),
        compiler_params=_params(_ARB2),
        name="fnet_group",
    )(x, mods, mods, g_row, cs)
    f = _fnet_pos(ab, None, L, 0, 512, 512)
    f = _fnet_pos(ab, f, LC, TL, LC, 512)
    tn = 512
    e_ops, e_specs, epilogue = _residual_ops(x, mods, 5, TM, tn, bias=b_out.reshape(1, D))
    return fused_matmul([f], [pl.BlockSpec((TM, D), lambda i, j: (i, 0))], None, D,
                        [w_out.astype(BF16)], [pl.BlockSpec((D, tn), lambda i, j: (0, j))],
                        e_ops, e_specs, epilogue, D, F32, TM, tn, T, "fnet_out")


def _conv_kernel(u_ref, w_ref, bdw_ref, lg_ref, lb_ref, *rest, seg):
    o_ref, pad_ref, y_ref = rest[-3:]
    n_seg = CONV_BLK // seg
    stride = seg + 2 * CONV_PAD
    zeros = jnp.zeros((CONV_PAD, D), F32)
    for s in range(n_seg):
        base = s * stride
        pad_ref[base:base + CONV_PAD, :] = zeros
        pad_ref[base + CONV_PAD:base + CONV_PAD + seg, :] = u_ref[s * seg:(s + 1) * seg, :]
        pad_ref[base + CONV_PAD + seg:base + stride, :] = zeros
    cw = 256
    rb = 64
    off0 = CONV_PAD - CONV_W // 2

    def chan_body(ci, carry):
        c0 = pl.multiple_of(ci * cw, cw)
        for s in range(n_seg):
            for r in range(seg // rb):
                start = s * stride + off0 + r * rb
                acc = jnp.zeros((rb, cw), F32)
                for k in range(CONV_W):
                    acc = acc + w_ref[k:k + 1, pl.ds(c0, cw)] * pad_ref[start + k:start + k + rb, pl.ds(c0, cw)]
                y_ref[s * seg + r * rb:s * seg + (r + 1) * rb, pl.ds(c0, cw)] = acc
        return carry

    lax.fori_loop(0, D // cw, chan_body, 0)
    y = y_ref[...] + bdw_ref[...]
    mu = jnp.mean(y, axis=-1, keepdims=True)
    var = jnp.mean(jnp.square(y - mu), axis=-1, keepdims=True)
    yn = (y - mu) * lax.rsqrt(var + EPS) * lg_ref[...] + lb_ref[...]
    o_ref[...] = _silu(yn).astype(o_ref.dtype)


def _conv_call(u, y_prev, w_dw, b_dw, ln_g, ln_b, seg, blk0, n_blk):
    in_specs = [pl.BlockSpec((CONV_BLK, D), lambda i: (blk0 + i, 0)),
                pl.BlockSpec((CONV_W, D), lambda i: (0, 0)),
                pl.BlockSpec((1, D), lambda i: (0, 0)),
                pl.BlockSpec((1, D), lambda i: (0, 0)),
                pl.BlockSpec((1, D), lambda i: (0, 0))]
    args = [u, w_dw, b_dw, ln_g, ln_b]
    aliases = {}
    if y_prev is not None:
        in_specs.append(pl.BlockSpec(memory_space=pl.ANY))
        args.append(y_prev)
        aliases = {5: 0}
    n_seg = CONV_BLK // seg
    return pl.pallas_call(
        functools.partial(_conv_kernel, seg=seg),
        grid=(n_blk,),
        in_specs=in_specs,
        out_specs=pl.BlockSpec((CONV_BLK, D), lambda i: (blk0 + i, 0)),
        out_shape=jax.ShapeDtypeStruct((T, D), BF16),
        scratch_shapes=[pltpu.VMEM((n_seg * (seg + 2 * CONV_PAD), D), F32),
                        pltpu.VMEM((CONV_BLK, D), F32)],
        input_output_aliases=aliases,
        compiler_params=_params(("arbitrary",)),
        name="conv_dw",
    )(*args)


def conv_layer(x, mods, g_row, w_pw1, b_pw1, w_dw, b_dw, ln_g, ln_b, w_pw2, b_pw2):
    a_ops, a_specs, prologue = _prenorm_ops(x, mods, 1, g_row, TM)
    tn = 512
    n_j = D // tn
    w1 = w_pw1.astype(BF16)
    b1 = b_pw1.reshape(1, 2 * D)

    def glu(accs, ba_ref, bg_ref):
        return (accs[0] + ba_ref[...]) * _sigmoid(accs[1] + bg_ref[...])

    u = fused_matmul(a_ops, a_specs, prologue, D, [w1, w1],
                     [pl.BlockSpec((D, tn), lambda i, j: (0, j)),
                      pl.BlockSpec((D, tn), lambda i, j: (0, n_j + j))],
                     [b1, b1],
                     [pl.BlockSpec((1, tn), lambda i, j: (0, j)),
                      pl.BlockSpec((1, tn), lambda i, j: (0, n_j + j))],
                     glu, D, F32, TM, tn, T, "conv_pw1")
    vecs = [b_dw.reshape(1, D), ln_g.reshape(1, D), ln_b.reshape(1, D)]
    y = _conv_call(u, None, w_dw, *vecs, GRID_W, 0, TL // CONV_BLK)
    y = _conv_call(u, y, w_dw, *vecs, LC, TL // CONV_BLK, TC // CONV_BLK)
    e_ops, e_specs, epilogue = _residual_ops(x, mods, 5, TM, tn, bias=b_pw2.reshape(1, D))
    return fused_matmul([y], [pl.BlockSpec((TM, D), lambda i, j: (i, 0))], None, D,
                        [w_pw2.astype(BF16)], [pl.BlockSpec((D, tn), lambda i, j: (0, j))],
                        e_ops, e_specs, epilogue, D, F32, TM, tn, T, "conv_pw2")


def kernel(x, c, ctx, c_ctx, w_mod, b_mod, norm_g, ffn_w_gate, ffn_w_up, ffn_w_down, gla_w_in, gla_w_gate_up, gla_b_gate, gla_g_head, gla_w_out, fnet_w_out, fnet_b_out, cm_w_pw1, cm_b_pw1, cm_w_dw, cm_b_dw, cm_ln_g, cm_ln_b, cm_w_pw2, cm_b_pw2, final_g):
    xs = jnp.concatenate([x.reshape(TL, D), ctx.reshape(TC, D)], axis=0)
    cvec = jnp.concatenate([c, c_ctx[None], jnp.zeros((MOD_ROWS - B - 1, D), F32)], axis=0)
    mods_all = adaln_all(cvec, w_mod, b_mod).reshape(DEPTH, MOD_ROWS * N_MOD, 1, D)
    wg = ffn_w_gate.astype(BF16)
    wu = ffn_w_up.astype(BF16)
    wd = ffn_w_down.astype(BF16)
    all_tiles = T // TM
    lat_tiles = TL // TM
    for i in range(DEPTH):
        kind = i % N_MIXERS
        j = i // N_MIXERS
        last = i == DEPTH - 1
        mods = mods_all[i]
        g_rows = [norm_g[i, k].reshape(1, D) for k in range(3)]
        xs = ffn_half(xs, mods, 0, g_rows[0], wg, wu, wd, i, 0, all_tiles)
        n_out = lat_tiles if last else all_tiles
        if kind == 0:
            xs = gla_layer(xs, mods, g_rows[1], gla_w_in[j], gla_w_gate_up[j], gla_b_gate[j],
                           gla_g_head[j], gla_w_out[j], n_out * TM)
        elif kind == 1:
            xs = fnet_layer(xs, mods, g_rows[1], fnet_w_out[j], fnet_b_out[j])
        else:
            xs = conv_layer(xs, mods, g_rows[1], cm_w_pw1[j], cm_b_pw1[j], cm_w_dw[j], cm_b_dw[j],
                            cm_ln_g[j], cm_ln_b[j], cm_w_pw2[j], cm_b_pw2[j])
        xs = ffn_half(xs, mods, 2, g_rows[2], wg, wu, wd, i, 1, n_out,
                      final_g=final_g.reshape(1, D) if last else None)
    return xs.reshape(B, L, D)
```

```python
import functools
import math

import jax
import jax.numpy as jnp
from jax import lax
from jax.experimental import pallas as pl
from jax.experimental.pallas import tpu as pltpu

F32 = jnp.float32
BF16 = jnp.bfloat16

D = 2048
B = 4
L = 4096
LC = 256
DEPTH = 4
N_MIXERS = 3
N_MOD = 9
DFF = 5632
EPS = 1e-6
GRID_W = 64

TL = B * L
TC = B * LC
T = TL + TC

H = 4
DK = D // (2 * H)
DV = D // H
RANK = 16
TAU = 16.0
CHUNK = 64
HK = H * DK
HV = H * DV
GLA_MAIN = 2 * HK + 2 * HV
GLA_BLK = 256
ZPAD = 128

FG = 8
GW = D // FG
CONV_W = 31
CONV_PAD = 16
CONV_BLK = 256

MOD_ROWS = 8
TM = 1024
TF = 512
ROW_PIECE = 16
ROW_GROUP = 128
VMEM_LIMIT = 60 * 1024 * 1024

_ARB2 = ("arbitrary", "arbitrary")


def _params(sem, limit=VMEM_LIMIT):
    return pltpu.CompilerParams(dimension_semantics=sem, vmem_limit_bytes=limit)


def _mod_row(t, tm):
    return jnp.minimum(t // (L // tm), B)


def _mod_spec(slot, tm):
    return pl.BlockSpec((None, 1, D), lambda i, j: (_mod_row(i, tm) * N_MOD + slot, 0, 0))


def _row_spec(n):
    return pl.BlockSpec((1, n), lambda i, j: (0, 0))


def _rms(x):
    return x * lax.rsqrt(jnp.mean(x * x, axis=-1, keepdims=True) + EPS)


def _prenorm(x, g, scale, shift):
    return (_rms(x) * g) * (1.0 + scale) + shift


def _rowwise(n_rows, piece):
    def trip(t, carry):
        for p in range(ROW_GROUP // ROW_PIECE):
            piece(pl.ds(pl.multiple_of(t * ROW_GROUP + p * ROW_PIECE, ROW_PIECE), ROW_PIECE))
        return carry

    lax.fori_loop(0, n_rows // ROW_GROUP, trip, 0)


def _prenorm_into(h_ref, x_ref, sh_ref, sc_ref, g_ref):
    def piece(rows):
        h_ref[rows, :] = _prenorm(x_ref[rows, :], g_ref[...], sc_ref[...], sh_ref[...]).astype(BF16)

    _rowwise(h_ref.shape[0], piece)


def _sigmoid(x):
    return 1.0 / (1.0 + jnp.exp(-x))


def _silu(x):
    return x * _sigmoid(x)


def _log_sigmoid(x):
    return jnp.minimum(x, 0.0) - jnp.log(1.0 + jnp.exp(-jnp.abs(x)))


def _adaln_kernel(c_ref, w_ref, b_ref, o_ref):
    s = _silu(c_ref[...]).astype(BF16)
    o_ref[...] = jnp.dot(s, w_ref[...].astype(BF16), preferred_element_type=F32) + b_ref[...]


def adaln_all(cvec, w_mod, b_mod):
    tn = 1024
    n = N_MOD * D
    return pl.pallas_call(
        _adaln_kernel,
        grid=(DEPTH, n // tn),
        in_specs=[
            pl.BlockSpec((MOD_ROWS, D), lambda i, j: (0, 0)),
            pl.BlockSpec((None, D, tn), lambda i, j: (i, 0, j)),
            pl.BlockSpec((None, 1, tn), lambda i, j: (i, 0, j)),
        ],
        out_specs=pl.BlockSpec((None, MOD_ROWS, tn), lambda i, j: (i, 0, j)),
        out_shape=jax.ShapeDtypeStruct((DEPTH, MOD_ROWS, n), F32),
        compiler_params=_params(_ARB2),
        name="adaln",
    )(cvec, w_mod, b_mod.reshape(DEPTH, 1, n))


def _ffn_kernel(x_ref, sh_ref, sc_ref, gt_ref, g_ref, wg_ref, wu_ref, wd_ref, *rest, n_j, final):
    if final:
        fg_ref, o_ref, h_ref = rest
    else:
        o_ref, h_ref = rest
    j = pl.program_id(1)

    @pl.when(j == 0)
    def _():
        _prenorm_into(h_ref, x_ref, sh_ref, sc_ref, g_ref)
        o_ref[...] = jnp.zeros_like(o_ref)

    h = h_ref[...]
    g = jnp.dot(h, wg_ref[...], preferred_element_type=F32)
    u = jnp.dot(h, wu_ref[...], preferred_element_type=F32)
    a = (_silu(g) * u).astype(BF16)
    o_ref[...] += jnp.dot(a, wd_ref[...], preferred_element_type=F32)

    @pl.when(j == n_j - 1)
    def _():
        out = x_ref[...] + (0.5 * gt_ref[...]) * o_ref[...]
        if final:
            out = _rms(out) * fg_ref[...]
        o_ref[...] = out


def ffn_half(x, mods, k, g_row, wg, wu, wd, layer, half, n_tiles, final_g=None):
    n_j = DFF // TF
    final = final_g is not None
    in_specs = [
        pl.BlockSpec((TM, D), lambda i, j: (i, 0), pipeline_mode=pl.Buffered(1)),
        _mod_spec(3 * k, TM), _mod_spec(3 * k + 1, TM), _mod_spec(3 * k + 2, TM),
        _row_spec(D),
        pl.BlockSpec((None, None, D, TF), lambda i, j: (layer, half, 0, j)),
        pl.BlockSpec((None, None, D, TF), lambda i, j: (layer, half, 0, j)),
        pl.BlockSpec((None, None, TF, D), lambda i, j: (layer, half, j, 0)),
    ]
    args = [x, mods, mods, mods, g_row, wg, wu, wd]
    if final:
        in_specs.append(_row_spec(D))
        args.append(final_g)
    return pl.pallas_call(
        functools.partial(_ffn_kernel, n_j=n_j, final=final),
        grid=(n_tiles, n_j),
        in_specs=in_specs,
        out_specs=pl.BlockSpec((TM, D), lambda i, j: (i, 0)),
        out_shape=jax.ShapeDtypeStruct((n_tiles * TM, D), F32),
        scratch_shapes=[pltpu.VMEM((TM, D), BF16)],
        compiler_params=_params(_ARB2),
        name="ffn_half",
    )(*args)


def _mm_kernel(*refs, n_a, n_w, n_e, prologue, epilogue):
    a_refs = refs[:n_a]
    w_refs = refs[n_a:n_a + n_w]
    e_refs = refs[n_a + n_w:n_a + n_w + n_e]
    if prologue is None:
        o_ref = refs[-1]
        h = a_refs[0][...]
    else:
        o_ref, h_ref = refs[-2:]

        @pl.when(pl.program_id(1) == 0)
        def _():
            prologue(h_ref, *a_refs)

        h = h_ref[...]
    accs = [jnp.dot(h, w[...], preferred_element_type=F32) for w in w_refs]
    o_ref[...] = epilogue(accs, *e_refs).astype(o_ref.dtype)


def fused_matmul(a_ops, a_specs, prologue, k_dim, w_ops, w_specs, e_ops, e_specs, epilogue,
                 n_out, out_dtype, tm, tn, n_rows, name):
    n_j = n_out // tn
    scratch = [] if prologue is None else [pltpu.VMEM((tm, k_dim), BF16)]
    return pl.pallas_call(
        functools.partial(_mm_kernel, n_a=len(a_ops), n_w=len(w_ops), n_e=len(e_ops),
                          prologue=prologue, epilogue=epilogue),
        grid=(n_rows // tm, n_j),
        in_specs=list(a_specs) + list(w_specs) + list(e_specs),
        out_specs=pl.BlockSpec((tm, tn), lambda i, j: (i, j)),
        out_shape=jax.ShapeDtypeStruct((n_rows, n_out), out_dtype),
        scratch_shapes=scratch,
        compiler_params=_params(_ARB2),
        name=name,
    )(*a_ops, *w_ops, *e_ops)


def _prenorm_ops(x, mods, k, g_row, tm):
    ops = [x, mods, mods, g_row]
    specs = [pl.BlockSpec((tm, D), lambda i, j: (i, 0)),
             _mod_spec(3 * k, tm), _mod_spec(3 * k + 1, tm), _row_spec(D)]

    return ops, specs, _prenorm_into


def _residual_ops(x, mods, slot, tm, tn, bias=None):
    ops = [x, mods]
    specs = [pl.BlockSpec((tm, tn), lambda i, j: (i, j)),
             pl.BlockSpec((None, 1, tn), lambda i, j: (_mod_row(i, tm) * N_MOD + slot, 0, j))]
    if bias is None:
        def epilogue(accs, x_ref, gt_ref):
            return x_ref[...] + gt_ref[...] * accs[0]
    else:
        ops.append(bias)
        specs.append(pl.BlockSpec((1, tn), lambda i, j: (0, j)))

        def epilogue(accs, x_ref, gt_ref, b_ref):
            return x_ref[...] + gt_ref[...] * (accs[0] + b_ref[...])
    return ops, specs, epilogue


def _gla_dir(q_ref, k_ref, v_ref, z_ref, wup_ref, bg_ref, s_ref, o_ref, reverse):
    n = GLA_BLK
    z = z_ref[...]
    z_hi = z.astype(BF16)
    z_lo = (z - z_hi.astype(F32)).astype(BF16)
    x = jnp.dot(jnp.concatenate([z_hi, z_hi, z_lo], axis=1), wup_ref[...],
                preferred_element_type=F32) + bg_ref[...]
    g = _log_sigmoid(x) * (1.0 / TAU)
    g1 = g.astype(BF16)
    r1 = g - g1.astype(F32)
    g2 = r1.astype(BF16)
    g3 = (r1 - g2.astype(F32)).astype(BF16)
    row = lax.broadcasted_iota(jnp.int32, (n, n), 0)
    col = lax.broadcasted_iota(jnp.int32, (n, n), 1)
    shift = int(math.log2(CHUNK))
    rc = jnp.right_shift(row, shift)
    cc = jnp.right_shift(col, shift)
    causal = (col >= row) if reverse else (col <= row)
    dist = (cc - rc) if reverse else (rc - cc)
    tri = jnp.where((rc == cc) & causal, 1.0, 0.0).astype(BF16)
    b3 = jnp.dot(tri, jnp.concatenate([g1, g2, g3], axis=1), preferred_element_type=F32)
    bcum_all = (b3[:, :HK] + b3[:, HK:2 * HK]) + b3[:, 2 * HK:]
    for h in range(H):
        _gla_head(q_ref, k_ref, v_ref, s_ref, o_ref, bcum_all[:, h * DK:(h + 1) * DK], h,
                  causal, dist, reverse)


def _gla_head(q_ref, k_ref, v_ref, s_ref, o_ref, bcum, h, causal, dist, reverse):
    n = GLA_BLK
    nc = n // CHUNK
    nt = (((1,), (1,)), ((), ()))
    kcols = slice(h * DK, (h + 1) * DK)
    vcols = slice(h * DV, (h + 1) * DV)
    i_ref = CHUNK // 2 if reverse else CHUNK // 2 - 1
    i_last = 0 if reverse else CHUNK - 1
    b_last = [bcum[c * CHUNK + i_last:c * CHUNK + i_last + 1, :] for c in range(nc)]
    b_ref = [bcum[c * CHUNK + i_ref:c * CHUNK + i_ref + 1, :] for c in range(nc)]
    scan = list(range(nc - 1, -1, -1)) if reverse else list(range(nc))
    zero = jnp.zeros((1, DK), F32)
    pfx, sfx, mid2, mid3 = {}, {}, {}, {}
    run = zero
    for c in scan:
        pfx[c] = run
        run = run + b_last[c]
    total = run
    run = zero
    for c in reversed(scan):
        sfx[c] = run
        run = run + b_last[c]
    for p, c in enumerate(scan):
        mid2[c] = b_last[scan[p - 1]] if p >= 2 else zero
        mid3[c] = b_last[scan[p - 1]] + b_last[scan[p - 2]] if p >= 3 else zero

    def rows(vals):
        return jnp.concatenate([jnp.broadcast_to(vals[c], (CHUNK, DK)) for c in range(nc)], axis=0)

    def exp_rows(vals):
        return rows({c: jnp.exp(vals[c]) for c in range(nc)})

    qs = q_ref[:, kcols].astype(F32) * (DK ** -0.5)
    kf = k_ref[:, kcols].astype(F32)
    v = v_ref[:, vcols]
    d_in = bcum - rows(b_ref)
    q_in = (qs * jnp.exp(d_in)).astype(BF16)
    k_in = (kf * jnp.exp(-d_in)).astype(BF16)
    q_it = qs * jnp.exp(bcum)
    k_st = kf * jnp.exp(rows(b_last) - bcum)
    q_hat = (q_it * exp_rows(pfx)).astype(BF16)
    k_hat = (k_st * exp_rows(sfx)).astype(BF16)
    q_cat = jnp.concatenate([q_it.astype(BF16), (q_it * exp_rows(mid2)).astype(BF16),
                             (q_it * exp_rows(mid3)).astype(BF16)], axis=0)
    a_same = lax.dot_general(q_in, k_in, nt, preferred_element_type=F32)
    a_prev = lax.dot_general(q_cat, k_st.astype(BF16), nt, preferred_element_type=F32)
    att = jnp.where((dist == 0) & causal, a_same, 0.0)
    for dd in range(1, nc):
        att = jnp.where(dist == dd, a_prev[(dd - 1) * n:dd * n], att)
    st_old = s_ref[h]
    o = jnp.dot(att.astype(BF16), v, preferred_element_type=F32)
    o = o + lax.dot_general(q_hat, st_old.astype(BF16), nt, preferred_element_type=F32)
    o_ref[:, vcols] = o.astype(o_ref.dtype)
    s_ref[h] = jnp.exp(total) * st_old + lax.dot_general(
        v, k_hat, (((0,), (0,)), ((), ())), preferred_element_type=F32)


def _gla_kernel(qf, kf, vf, zf, qb, kb, vb, zb, wf, wb, bf, bb, of_ref, ob_ref, sf_ref, sb_ref):
    @pl.when(pl.program_id(1) == 0)
    def _():
        sf_ref[...] = jnp.zeros_like(sf_ref)
        sb_ref[...] = jnp.zeros_like(sb_ref)

    _gla_dir(qf, kf, vf, zf, wf, bf, sf_ref, of_ref, reverse=False)
    _gla_dir(qb, kb, vb, zb, wb, bb, sb_ref, ob_ref, reverse=True)


def gla_scan(p, z, wup_pad, b_gate):
    nl = L // GLA_BLK
    ctx0 = TL // GLA_BLK

    def row_f(b, s):
        return jnp.where(s == 0, ctx0 + b, b * nl + s - 1)

    def row_b(b, s):
        return jnp.where(s == 0, ctx0 + b, b * nl + nl - s)

    def specs(row):
        return [
            pl.BlockSpec((GLA_BLK, HK), lambda b, s: (row(b, s), 0)),
            pl.BlockSpec((GLA_BLK, HK), lambda b, s: (row(b, s), 1)),
            pl.BlockSpec((GLA_BLK, HV), lambda b, s: (row(b, s), (2 * HK) // HV)),
            pl.BlockSpec((GLA_BLK, ZPAD), lambda b, s: (row(b, s), 0)),
        ]

    in_specs = specs(row_f) + specs(row_b) + [
        pl.BlockSpec((None, 3 * ZPAD, HK), lambda b, s: (0, 0, 0)),
        pl.BlockSpec((None, 3 * ZPAD, HK), lambda b, s: (1, 0, 0)),
        pl.BlockSpec((None, 1, HK), lambda b, s: (0, 0, 0)),
        pl.BlockSpec((None, 1, HK), lambda b, s: (1, 0, 0)),
    ]
    out_sds = jax.ShapeDtypeStruct((T, HV), BF16)
    return pl.pallas_call(
        _gla_kernel,
        grid=(B, nl + 1),
        in_specs=in_specs,
        out_specs=[pl.BlockSpec((GLA_BLK, HV), lambda b, s: (row_f(b, s), 0)),
                   pl.BlockSpec((GLA_BLK, HV), lambda b, s: (row_b(b, s), 0))],
        out_shape=[out_sds, out_sds],
        scratch_shapes=[pltpu.VMEM((H, DV, DK), F32), pltpu.VMEM((H, DV, DK), F32)],
        compiler_params=_params(_ARB2),
        name="gla_scan",
    )(p, p, p, z, p, p, p, z, wup_pad, wup_pad, b_gate, b_gate)


def gla_layer(x, mods, g_row, w_in, w_gate_up, b_gate, g_head, w_out, n_out_rows):
    a_ops, a_specs, prologue = _prenorm_ops(x, mods, 1, g_row, TM)
    w_main = w_in[:, :GLA_MAIN].astype(BF16)
    w_z = jnp.pad(w_in[:, GLA_MAIN:], ((0, 0), (0, ZPAD - 2 * RANK))).astype(BF16)
    plain = lambda accs: accs[0]
    tn = 512
    p = fused_matmul(a_ops, a_specs, prologue, D, [w_main],
                     [pl.BlockSpec((D, tn), lambda i, j: (0, j))], [], [], plain,
                     GLA_MAIN, BF16, TM, tn, T, "gla_in")
    z = fused_matmul(a_ops, a_specs, prologue, D, [w_z],
                     [pl.BlockSpec((D, ZPAD), lambda i, j: (0, j))], [], [], plain,
                     ZPAD, F32, TM, ZPAD, T, "gla_rank")
    wup_pad = jnp.zeros((2, ZPAD, HK), F32)
    wup_pad = wup_pad.at[0, :RANK].set(w_gate_up[0]).at[1, RANK:2 * RANK].set(w_gate_up[1])
    wup_hi = wup_pad.astype(BF16)
    wup_lo = (wup_pad - wup_hi.astype(F32)).astype(BF16)
    wup_cat = jnp.concatenate([wup_hi, wup_lo, wup_hi], axis=1)
    o_f, o_b = gla_scan(p, z, wup_cat, b_gate.reshape(2, 1, HK))

    def out_prologue(h_ref, of_ref, ob_ref, r_ref, gh_ref):
        def piece(rows):
            for h in range(H):
                cols = slice(h * DV, (h + 1) * DV)
                o = of_ref[rows, cols].astype(F32) + ob_ref[rows, cols].astype(F32)
                gate = _silu(r_ref[rows, cols].astype(F32))
                h_ref[rows, cols] = (_rms(o) * gh_ref[...] * gate).astype(BF16)

        _rowwise(h_ref.shape[0], piece)

    tmo = TM
    o_ops = [o_f, o_b, p, g_head.reshape(1, DV)]
    o_specs = [pl.BlockSpec((tmo, HV), lambda i, j: (i, 0)),
               pl.BlockSpec((tmo, HV), lambda i, j: (i, 0)),
               pl.BlockSpec((tmo, HV), lambda i, j: (i, (2 * HK + HV) // HV)),
               _row_spec(DV)]
    e_ops, e_specs, epilogue = _residual_ops(x, mods, 5, tmo, tn)
    return fused_matmul(o_ops, o_specs, out_prologue, HV, [w_out.astype(BF16)],
                        [pl.BlockSpec((HV, tn), lambda i, j: (0, j))], e_ops, e_specs, epilogue,
                        D, F32, tmo, tn, n_out_rows, "gla_out")


def _fnet_group_kernel(x_ref, sh_ref, sc_ref, g_ref, cs_ref, o_ref, h_ref):
    _prenorm_into(h_ref, x_ref, sh_ref, sc_ref, g_ref)
    cs = cs_ref[...]
    for g in range(FG):
        ab = jnp.dot(h_ref[:, g * GW:(g + 1) * GW], cs, preferred_element_type=F32).astype(BF16)
        o_ref[:, g * GW:(g + 1) * GW] = ab[:, :GW]
        o_ref[:, D + g * GW:D + (g + 1) * GW] = ab[:, GW:]


def _fnet_pos_kernel(c_ref, s_ref, a_ref, b_ref, *rest, scale):
    o_ref = rest[-1]
    acc = jnp.dot(c_ref[...], a_ref[...], preferred_element_type=F32)
    acc = acc - jnp.dot(s_ref[...], b_ref[...], preferred_element_type=F32)
    o_ref[...] = (acc * scale).astype(o_ref.dtype)


def _dft_angles(rows, cols, n):
    return ((rows[:, None] * cols[None, :]) % n).astype(F32) * (2.0 * math.pi / n)


def _dft_tables(n):
    idx = jnp.arange(n, dtype=jnp.int32)
    m = 64
    if n <= m * m // 16:
        ang = _dft_angles(idx, idx, n)
        return jnp.cos(ang), jnp.sin(ang)
    a = _dft_angles(idx, jnp.arange(n // m, dtype=jnp.int32), n // m)
    b = _dft_angles(idx, jnp.arange(m, dtype=jnp.int32), n)
    ca, sa = jnp.cos(a)[:, :, None], jnp.sin(a)[:, :, None]
    cb, sb = jnp.cos(b)[:, None, :], jnp.sin(b)[:, None, :]
    return (ca * cb - sa * sb).reshape(n, n), (sa * cb + ca * sb).reshape(n, n)


def _fnet_pos(ab, f_prev, seq, row0, tm, tn):
    cos_t, sin_t = _dft_tables(seq)
    cos_t, sin_t = cos_t.astype(BF16), sin_t.astype(BF16)
    blk0 = row0 // seq
    n_m = seq // tm
    n_n = D // tn
    in_specs = [
        pl.BlockSpec((tm, seq), lambda b, n, m: (m, 0)),
        pl.BlockSpec((tm, seq), lambda b, n, m: (m, 0)),
        pl.BlockSpec((seq, tn), lambda b, n, m: (blk0 + b, n)),
        pl.BlockSpec((seq, tn), lambda b, n, m: (blk0 + b, n_n + n)),
    ]
    args = [cos_t, sin_t, ab, ab]
    aliases = {}
    if f_prev is not None:
        in_specs.append(pl.BlockSpec(memory_space=pl.ANY))
        args.append(f_prev)
        aliases = {4: 0}
    return pl.pallas_call(
        functools.partial(_fnet_pos_kernel, scale=1.0 / math.sqrt(seq * GW)),
        grid=(B, n_n, n_m),
        in_specs=in_specs,
        out_specs=pl.BlockSpec((tm, tn), lambda b, n, m: ((blk0 + b) * n_m + m, n)),
        out_shape=jax.ShapeDtypeStruct((T, D), BF16),
        input_output_aliases=aliases,
        compiler_params=_params(("arbitrary",) * 3),
        name="fnet_pos",
    )(*args)


def fnet_layer(x, mods, g_row, w_out, b_out):
    cos_g, sin_g = _dft_tables(GW)
    cs = jnp.concatenate([cos_g, sin_g], axis=1).astype(BF16)
    tmg = 512
    ab = pl.pallas_call(
        _fnet_group_kernel,
        grid=(T // tmg, 1),
        in_specs=[pl.BlockSpec((tmg, D), lambda i, j: (i, 0)),
                  _mod_spec(3, tmg), _mod_spec(4, tmg), _row_spec(D),
                  pl.BlockSpec((GW, 2 * GW), lambda i, j: (0, 0))],
        out_specs=pl.BlockSpec((tmg, 2 * D), lambda i, j: (i, 0)),
        out_shape=jax.ShapeDtypeStruct((T, 2 * D), BF16),
        scratch_shapes=[pltpu.VMEM((tmg, D), BF16)],
        compiler_params=_params(_ARB2),
        name="fnet_group",
    )(x, mods, mods, g_row, cs)
    f = _fnet_pos(ab, None, L, 0, 512, 512)
    f = _fnet_pos(ab, f, LC, TL, LC, 512)
    tn = 512
    e_ops, e_specs, epilogue = _residual_ops(x, mods, 5, TM, tn, bias=b_out.reshape(1, D))
    return fused_matmul([f], [pl.BlockSpec((TM, D), lambda i, j: (i, 0))], None, D,
                        [w_out.astype(BF16)], [pl.BlockSpec((D, tn), lambda i, j: (0, j))],
                        e_ops, e_specs, epilogue, D, F32, TM, tn, T, "fnet_out")


def _conv_kernel(u_ref, w_ref, bdw_ref, lg_ref, lb_ref, *rest, seg):
    o_ref, pad_ref, y_ref = rest[-3:]
    n_seg = CONV_BLK // seg
    stride = seg + 2 * CONV_PAD
    zeros = jnp.zeros((CONV_PAD, D), F32)
    for s in range(n_seg):
        base = s * stride
        pad_ref[base:base + CONV_PAD, :] = zeros
        pad_ref[base + CONV_PAD:base + CONV_PAD + seg, :] = u_ref[s * seg:(s + 1) * seg, :]
        pad_ref[base + CONV_PAD + seg:base + stride, :] = zeros
    cw = 256
    rb = 64
    off0 = CONV_PAD - CONV_W // 2
    win = rb + 2 * CONV_PAD
    sub = 8

    def chan_body(ci, carry):
        c0 = pl.multiple_of(ci * cw, cw)
        for s in range(n_seg):
            for r in range(seg // rb):
                start = s * stride + r * rb
                window = pad_ref[start:start + win, pl.ds(c0, cw)]
                acc = jnp.zeros((rb, cw), F32)
                for res in range(sub):
                    shifted = window if res == 0 else pltpu.roll(window, win - res, axis=0)
                    for k in range(CONV_W):
                        if (k + off0) % sub == res:
                            a = k + off0 - res
                            acc = acc + w_ref[k:k + 1, pl.ds(c0, cw)] * shifted[a:a + rb, :]
                y_ref[s * seg + r * rb:s * seg + (r + 1) * rb, pl.ds(c0, cw)] = acc
        return carry

    lax.fori_loop(0, D // cw, chan_body, 0)

    def norm_piece(rows):
        y = y_ref[rows, :] + bdw_ref[...]
        mu = jnp.mean(y, axis=-1, keepdims=True)
        var = jnp.mean(jnp.square(y - mu), axis=-1, keepdims=True)
        yn = (y - mu) * lax.rsqrt(var + EPS) * lg_ref[...] + lb_ref[...]
        o_ref[rows, :] = _silu(yn).astype(o_ref.dtype)

    _rowwise(CONV_BLK, norm_piece)


def _conv_call(u, y_prev, w_dw, b_dw, ln_g, ln_b, seg, blk0, n_blk):
    in_specs = [pl.BlockSpec((CONV_BLK, D), lambda i: (blk0 + i, 0)),
                pl.BlockSpec((CONV_W, D), lambda i: (0, 0)),
                pl.BlockSpec((1, D), lambda i: (0, 0)),
                pl.BlockSpec((1, D), lambda i: (0, 0)),
                pl.BlockSpec((1, D), lambda i: (0, 0))]
    args = [u, w_dw, b_dw, ln_g, ln_b]
    aliases = {}
    if y_prev is not None:
        in_specs.append(pl.BlockSpec(memory_space=pl.ANY))
        args.append(y_prev)
        aliases = {5: 0}
    n_seg = CONV_BLK // seg
    return pl.pallas_call(
        functools.partial(_conv_kernel, seg=seg),
        grid=(n_blk,),
        in_specs=in_specs,
        out_specs=pl.BlockSpec((CONV_BLK, D), lambda i: (blk0 + i, 0)),
        out_shape=jax.ShapeDtypeStruct((T, D), BF16),
        scratch_shapes=[pltpu.VMEM((n_seg * (seg + 2 * CONV_PAD), D), F32),
                        pltpu.VMEM((CONV_BLK, D), F32)],
        input_output_aliases=aliases,
        compiler_params=_params(("arbitrary",)),
        name="conv_dw",
    )(*args)


def conv_layer(x, mods, g_row, w_pw1, b_pw1, w_dw, b_dw, ln_g, ln_b, w_pw2, b_pw2):
    a_ops, a_specs, prologue = _prenorm_ops(x, mods, 1, g_row, TM)
    tn = 512
    n_j = D // tn
    w1 = w_pw1.astype(BF16)
    b1 = b_pw1.reshape(1, 2 * D)

    def glu(accs, ba_ref, bg_ref):
        return (accs[0] + ba_ref[...]) * _sigmoid(accs[1] + bg_ref[...])

    u = fused_matmul(a_ops, a_specs, prologue, D, [w1, w1],
                     [pl.BlockSpec((D, tn), lambda i, j: (0, j)),
                      pl.BlockSpec((D, tn), lambda i, j: (0, n_j + j))],
                     [b1, b1],
                     [pl.BlockSpec((1, tn), lambda i, j: (0, j)),
                      pl.BlockSpec((1, tn), lambda i, j: (0, n_j + j))],
                     glu, D, F32, TM, tn, T, "conv_pw1")
    vecs = [b_dw.reshape(1, D), ln_g.reshape(1, D), ln_b.reshape(1, D)]
    y = _conv_call(u, None, w_dw, *vecs, GRID_W, 0, TL // CONV_BLK)
    y = _conv_call(u, y, w_dw, *vecs, LC, TL // CONV_BLK, TC // CONV_BLK)
    e_ops, e_specs, epilogue = _residual_ops(x, mods, 5, TM, tn, bias=b_pw2.reshape(1, D))
    return fused_matmul([y], [pl.BlockSpec((TM, D), lambda i, j: (i, 0))], None, D,
                        [w_pw2.astype(BF16)], [pl.BlockSpec((D, tn), lambda i, j: (0, j))],
                        e_ops, e_specs, epilogue, D, F32, TM, tn, T, "conv_pw2")


def kernel(x, c, ctx, c_ctx, w_mod, b_mod, norm_g, ffn_w_gate, ffn_w_up, ffn_w_down, gla_w_in, gla_w_gate_up, gla_b_gate, gla_g_head, gla_w_out, fnet_w_out, fnet_b_out, cm_w_pw1, cm_b_pw1, cm_w_dw, cm_b_dw, cm_ln_g, cm_ln_b, cm_w_pw2, cm_b_pw2, final_g):
    xs = jnp.concatenate([x.reshape(TL, D), ctx.reshape(TC, D)], axis=0)
    cvec = jnp.concatenate([c, c_ctx[None], jnp.zeros((MOD_ROWS - B - 1, D), F32)], axis=0)
    mods_all = adaln_all(cvec, w_mod, b_mod).reshape(DEPTH, MOD_ROWS * N_MOD, 1, D)
    wg = ffn_w_gate.astype(BF16)
    wu = ffn_w_up.astype(BF16)
    wd = ffn_w_down.astype(BF16)
    all_tiles = T // TM
    lat_tiles = TL // TM
    for i in range(DEPTH):
        kind = i % N_MIXERS
        j = i // N_MIXERS
        last = i == DEPTH - 1
        mods = mods_all[i]
        g_rows = [norm_g[i, k].reshape(1, D) for k in range(3)]
        xs = ffn_half(xs, mods, 0, g_rows[0], wg, wu, wd, i, 0, all_tiles)
        n_out = lat_tiles if last else all_tiles
        if kind == 0:
            xs = gla_layer(xs, mods, g_rows[1], gla_w_in[j], gla_w_gate_up[j], gla_b_gate[j],
                           gla_g_head[j], gla_w_out[j], n_out * TM)
        elif kind == 1:
            xs = fnet_layer(xs, mods, g_rows[1], fnet_w_out[j], fnet_b_out[j])
        else:
            xs = conv_layer(xs, mods, g_rows[1], cm_w_pw1[j], cm_b_pw1[j], cm_w_dw[j], cm_b_dw[j],
                            cm_ln_g[j], cm_ln_b[j], cm_w_pw2[j], cm_b_pw2[j])
        xs = ffn_half(xs, mods, 2, g_rows[2], wg, wu, wd, i, 1, n_out,
                      final_g=final_g.reshape(1, D) if last else None)
    return xs.reshape(B, L, D)
```

```python
import functools
import math

import jax
import jax.numpy as jnp
from jax import lax
from jax.experimental import pallas as pl
from jax.experimental.pallas import tpu as pltpu

F32 = jnp.float32
BF16 = jnp.bfloat16

D = 2048
B = 4
L = 4096
LC = 256
DEPTH = 4
N_MIXERS = 3
N_MOD = 9
DFF = 5632
EPS = 1e-6
GRID_W = 64

TL = B * L
TC = B * LC
T = TL + TC

H = 4
DK = D // (2 * H)
DV = D // H
RANK = 16
TAU = 16.0
CHUNK = 64
HK = H * DK
HV = H * DV
GLA_MAIN = 2 * HK + 2 * HV
GLA_BLK = 256
ZPAD = 128

FG = 8
GW = D // FG
CONV_W = 31
CONV_PAD = 16
CONV_BLK = 256

MOD_ROWS = 8
TM = 1024
TF = 512
ROW_PIECE = 16
ROW_GROUP = 128
CAST_STEPS = 64
VMEM_LIMIT = 60 * 1024 * 1024

_ARB2 = ("arbitrary", "arbitrary")


def _params(sem, limit=VMEM_LIMIT):
    return pltpu.CompilerParams(dimension_semantics=sem, vmem_limit_bytes=limit)


def _mod_row(t, tm):
    return jnp.minimum(t // (L // tm), B)


def _mod_spec(slot, tm):
    return pl.BlockSpec((None, 1, D), lambda i, j: (_mod_row(i, tm) * N_MOD + slot, 0, 0))


def _row_spec(n):
    return pl.BlockSpec((1, n), lambda i, j: (0, 0))


def _rms(x):
    return x * lax.rsqrt(jnp.mean(x * x, axis=-1, keepdims=True) + EPS)


def _prenorm(x, g, scale, shift):
    return (_rms(x) * g) * (1.0 + scale) + shift


def _rowwise(n_rows, piece):
    def trip(t, carry):
        for p in range(ROW_GROUP // ROW_PIECE):
            piece(pl.ds(pl.multiple_of(t * ROW_GROUP + p * ROW_PIECE, ROW_PIECE), ROW_PIECE))
        return carry

    lax.fori_loop(0, n_rows // ROW_GROUP, trip, 0)


def _prenorm_into(h_ref, x_ref, sh_ref, sc_ref, g_ref):
    def piece(rows):
        h_ref[rows, :] = _prenorm(x_ref[rows, :], g_ref[...], sc_ref[...], sh_ref[...]).astype(BF16)

    _rowwise(h_ref.shape[0], piece)


def _sigmoid(x):
    return 1.0 / (1.0 + jnp.exp(-x))


def _silu(x):
    return x * _sigmoid(x)


def _log_sigmoid(x):
    return jnp.minimum(x, 0.0) - jnp.log(1.0 + jnp.exp(-jnp.abs(x)))


def _adaln_kernel(c_ref, w_ref, b_ref, o_ref):
    s = _silu(c_ref[...]).astype(BF16)
    o_ref[...] = jnp.dot(s, w_ref[...].astype(BF16), preferred_element_type=F32) + b_ref[...]


def adaln_all(cvec, w_mod, b_mod):
    tn = 1024
    n = N_MOD * D
    return pl.pallas_call(
        _adaln_kernel,
        grid=(DEPTH, n // tn),
        in_specs=[
            pl.BlockSpec((MOD_ROWS, D), lambda i, j: (0, 0)),
            pl.BlockSpec((None, D, tn), lambda i, j: (i, 0, j)),
            pl.BlockSpec((None, 1, tn), lambda i, j: (i, 0, j)),
        ],
        out_specs=pl.BlockSpec((None, MOD_ROWS, tn), lambda i, j: (i, 0, j)),
        out_shape=jax.ShapeDtypeStruct((DEPTH, MOD_ROWS, n), F32),
        compiler_params=_params(_ARB2),
        name="adaln",
    )(cvec, w_mod, b_mod.reshape(DEPTH, 1, n))


def _cast_kernel(*refs):
    n = len(refs) // 2
    for src, dst in zip(refs[:n], refs[n:]):
        dst[...] = src[...].astype(dst.dtype)


def cast_bf16(*ws):
    shape = ws[0].shape
    cols = shape[-1]
    rows = math.prod(shape[:-1])
    spec = pl.BlockSpec((rows // CAST_STEPS, cols), lambda i: (i, 0))
    outs = pl.pallas_call(
        _cast_kernel,
        grid=(CAST_STEPS,),
        in_specs=[spec] * len(ws),
        out_specs=[spec] * len(ws),
        out_shape=[jax.ShapeDtypeStruct((rows, cols), BF16)] * len(ws),
        compiler_params=_params(("arbitrary",)),
        name="cast_bf16",
    )(*[w.reshape(rows, cols) for w in ws])
    return [o.reshape(shape) for o in outs]


def _ffn_kernel(*refs, n_i, n_j, n_head, final):
    head_hbm, tail_hbm, sh_ref, sc_ref, gt_ref, g_ref, wg_ref, wu_ref, wd_ref = refs[:9]
    if final:
        fg_ref, o_ref, xbuf, h_ref, sem = refs[9:]
    else:
        o_ref, xbuf, h_ref, sem = refs[9:]
    i = pl.program_id(0)
    j = pl.program_id(1)

    def x_copy(tile, act):
        @pl.when(tile < n_head)
        def _():
            act(pltpu.make_async_copy(head_hbm.at[pl.ds(tile * TM, TM), :], xbuf, sem))

        @pl.when(tile >= n_head)
        def _():
            act(pltpu.make_async_copy(tail_hbm.at[pl.ds((tile - n_head) * TM, TM), :], xbuf, sem))

    @pl.when((i == 0) & (j == 0))
    def _():
        x_copy(0, lambda c: c.start())

    @pl.when(j == 0)
    def _():
        x_copy(i, lambda c: c.wait())
        _prenorm_into(h_ref, xbuf, sh_ref, sc_ref, g_ref)
        o_ref[...] = xbuf[...]

    @pl.when((j == 1) & (i + 1 < n_i))
    def _():
        x_copy(i + 1, lambda c: c.start())

    h = h_ref[...]
    g = jnp.dot(h, wg_ref[...], preferred_element_type=F32)
    u = jnp.dot(h, wu_ref[...], preferred_element_type=F32)
    a = (_silu(g) * u).astype(BF16)
    o_ref[...] += (0.5 * gt_ref[...]) * jnp.dot(a, wd_ref[...], preferred_element_type=F32)

    if final:
        @pl.when(j == n_j - 1)
        def _():
            o_ref[...] = _rms(o_ref[...]) * fg_ref[...]


def ffn_half(x, mods, k, g_row, wg, wu, wd, layer, half, n_tiles, final_g=None, x_tail=None):
    n_j = DFF // TF
    assert n_j >= 2
    final = final_g is not None
    n_head = n_tiles if x_tail is None else x.shape[0] // TM
    if x_tail is None:
        x_tail = x
    in_specs = [
        pl.BlockSpec(memory_space=pl.ANY), pl.BlockSpec(memory_space=pl.ANY),
        _mod_spec(3 * k, TM), _mod_spec(3 * k + 1, TM), _mod_spec(3 * k + 2, TM),
        _row_spec(D),
        pl.BlockSpec((None, None, D, TF), lambda i, j: (layer, half, 0, j)),
        pl.BlockSpec((None, None, D, TF), lambda i, j: (layer, half, 0, j)),
        pl.BlockSpec((None, None, TF, D), lambda i, j: (layer, half, j, 0)),
    ]
    args = [x, x_tail, mods, mods, mods, g_row, wg, wu, wd]
    if final:
        in_specs.append(_row_spec(D))
        args.append(final_g)
    return pl.pallas_call(
        functools.partial(_ffn_kernel, n_i=n_tiles, n_j=n_j, n_head=n_head, final=final),
        grid=(n_tiles, n_j),
        in_specs=in_specs,
        out_specs=pl.BlockSpec((TM, D), lambda i, j: (i, 0)),
        out_shape=jax.ShapeDtypeStruct((n_tiles * TM, D), F32),
        scratch_shapes=[pltpu.VMEM((TM, D), F32), pltpu.VMEM((TM, D), BF16),
                        pltpu.SemaphoreType.DMA(())],
        compiler_params=_params(_ARB2),
        name="ffn_half",
    )(*args)


def _mm_kernel(*refs, n_a, n_w, n_e, prologue, epilogue):
    a_refs = refs[:n_a]
    w_refs = refs[n_a:n_a + n_w]
    e_refs = refs[n_a + n_w:n_a + n_w + n_e]
    if prologue is None:
        o_ref = refs[-1]
        h = a_refs[0][...]
    else:
        o_ref, h_ref = refs[-2:]

        @pl.when(pl.program_id(1) == 0)
        def _():
            prologue(h_ref, *a_refs)

        h = h_ref[...]
    accs = [jnp.dot(h, w[...], preferred_element_type=F32) for w in w_refs]
    o_ref[...] = epilogue(accs, *e_refs).astype(o_ref.dtype)


def fused_matmul(a_ops, a_specs, prologue, k_dim, w_ops, w_specs, e_ops, e_specs, epilogue,
                 n_out, out_dtype, tm, tn, n_rows, name):
    n_j = n_out // tn
    scratch = [] if prologue is None else [pltpu.VMEM((tm, k_dim), BF16)]
    return pl.pallas_call(
        functools.partial(_mm_kernel, n_a=len(a_ops), n_w=len(w_ops), n_e=len(e_ops),
                          prologue=prologue, epilogue=epilogue),
        grid=(n_rows // tm, n_j),
        in_specs=list(a_specs) + list(w_specs) + list(e_specs),
        out_specs=pl.BlockSpec((tm, tn), lambda i, j: (i, j)),
        out_shape=jax.ShapeDtypeStruct((n_rows, n_out), out_dtype),
        scratch_shapes=scratch,
        compiler_params=_params(_ARB2),
        name=name,
    )(*a_ops, *w_ops, *e_ops)


def _prenorm_ops(x, mods, k, g_row, tm):
    ops = [x, mods, mods, g_row]
    specs = [pl.BlockSpec((tm, D), lambda i, j: (i, 0)),
             _mod_spec(3 * k, tm), _mod_spec(3 * k + 1, tm), _row_spec(D)]

    return ops, specs, _prenorm_into


def _residual_ops(x, mods, slot, tm, tn, bias=None):
    ops = [x, mods]
    specs = [pl.BlockSpec((tm, tn), lambda i, j: (i, j)),
             pl.BlockSpec((None, 1, tn), lambda i, j: (_mod_row(i, tm) * N_MOD + slot, 0, j))]
    if bias is None:
        def epilogue(accs, x_ref, gt_ref):
            return x_ref[...] + gt_ref[...] * accs[0]
    else:
        ops.append(bias)
        specs.append(pl.BlockSpec((1, tn), lambda i, j: (0, j)))

        def epilogue(accs, x_ref, gt_ref, b_ref):
            return x_ref[...] + gt_ref[...] * (accs[0] + b_ref[...])
    return ops, specs, epilogue


def _gla_dir(q_ref, k_ref, v_ref, z_ref, wup_ref, bg_ref, s_ref, o_ref, reverse):
    n = GLA_BLK
    z = z_ref[...]
    z_hi = z.astype(BF16)
    z_lo = (z - z_hi.astype(F32)).astype(BF16)
    x = jnp.dot(jnp.concatenate([z_hi, z_hi, z_lo], axis=1), wup_ref[...],
                preferred_element_type=F32) + bg_ref[...]
    g = _log_sigmoid(x) * (1.0 / TAU)
    g1 = g.astype(BF16)
    r1 = g - g1.astype(F32)
    g2 = r1.astype(BF16)
    g3 = (r1 - g2.astype(F32)).astype(BF16)
    row = lax.broadcasted_iota(jnp.int32, (n, n), 0)
    col = lax.broadcasted_iota(jnp.int32, (n, n), 1)
    shift = int(math.log2(CHUNK))
    rc = jnp.right_shift(row, shift)
    cc = jnp.right_shift(col, shift)
    causal = (col >= row) if reverse else (col <= row)
    dist = (cc - rc) if reverse else (rc - cc)
    tri = jnp.where((rc == cc) & causal, 1.0, 0.0).astype(BF16)
    b3 = jnp.dot(tri, jnp.concatenate([g1, g2, g3], axis=1), preferred_element_type=F32)
    bcum_all = (b3[:, :HK] + b3[:, HK:2 * HK]) + b3[:, 2 * HK:]
    for h in range(H):
        _gla_head(q_ref, k_ref, v_ref, s_ref, o_ref, bcum_all[:, h * DK:(h + 1) * DK], h,
                  causal, dist, reverse)


def _gla_head(q_ref, k_ref, v_ref, s_ref, o_ref, bcum, h, causal, dist, reverse):
    n = GLA_BLK
    nc = n // CHUNK
    nt = (((1,), (1,)), ((), ()))
    kcols = slice(h * DK, (h + 1) * DK)
    vcols = slice(h * DV, (h + 1) * DV)
    i_ref = CHUNK // 2 if reverse else CHUNK // 2 - 1
    i_last = 0 if reverse else CHUNK - 1
    b_last = [bcum[c * CHUNK + i_last:c * CHUNK + i_last + 1, :] for c in range(nc)]
    b_ref = [bcum[c * CHUNK + i_ref:c * CHUNK + i_ref + 1, :] for c in range(nc)]
    scan = list(range(nc - 1, -1, -1)) if reverse else list(range(nc))
    zero = jnp.zeros((1, DK), F32)
    pfx, sfx, mid2, mid3 = {}, {}, {}, {}
    run = zero
    for c in scan:
        pfx[c] = run
        run = run + b_last[c]
    total = run
    run = zero
    for c in reversed(scan):
        sfx[c] = run
        run = run + b_last[c]
    for p, c in enumerate(scan):
        mid2[c] = b_last[scan[p - 1]] if p >= 2 else zero
        mid3[c] = b_last[scan[p - 1]] + b_last[scan[p - 2]] if p >= 3 else zero

    def rows(vals):
        return jnp.concatenate([jnp.broadcast_to(vals[c], (CHUNK, DK)) for c in range(nc)], axis=0)

    def exp_rows(vals):
        return rows({c: jnp.exp(vals[c]) for c in range(nc)})

    qs = q_ref[:, kcols].astype(F32) * (DK ** -0.5)
    kf = k_ref[:, kcols].astype(F32)
    v = v_ref[:, vcols]
    d_in = bcum - rows(b_ref)
    q_in = (qs * jnp.exp(d_in)).astype(BF16)
    k_in = (kf * jnp.exp(-d_in)).astype(BF16)
    q_it = qs * jnp.exp(bcum)
    k_st = kf * jnp.exp(rows(b_last) - bcum)
    q_hat = (q_it * exp_rows(pfx)).astype(BF16)
    k_hat = (k_st * exp_rows(sfx)).astype(BF16)
    q_cat = jnp.concatenate([q_it.astype(BF16), (q_it * exp_rows(mid2)).astype(BF16),
                             (q_it * exp_rows(mid3)).astype(BF16)], axis=0)
    a_same = lax.dot_general(q_in, k_in, nt, preferred_element_type=F32)
    a_prev = lax.dot_general(q_cat, k_st.astype(BF16), nt, preferred_element_type=F32)
    att = jnp.where((dist == 0) & causal, a_same, 0.0)
    for dd in range(1, nc):
        att = jnp.where(dist == dd, a_prev[(dd - 1) * n:dd * n], att)
    st_old = s_ref[h]
    o = jnp.dot(att.astype(BF16), v, preferred_element_type=F32)
    o = o + lax.dot_general(q_hat, st_old.astype(BF16), nt, preferred_element_type=F32)
    o_ref[:, vcols] = o.astype(o_ref.dtype)
    s_ref[h] = jnp.exp(total) * st_old + lax.dot_general(
        v, k_hat, (((0,), (0,)), ((), ())), preferred_element_type=F32)


def _gla_kernel(qf, kf, vf, zf, qb, kb, vb, zb, wf, wb, bf, bb, of_ref, ob_ref, sf_ref, sb_ref):
    @pl.when(pl.program_id(1) == 0)
    def _():
        sf_ref[...] = jnp.zeros_like(sf_ref)
        sb_ref[...] = jnp.zeros_like(sb_ref)

    _gla_dir(qf, kf, vf, zf, wf, bf, sf_ref, of_ref, reverse=False)
    _gla_dir(qb, kb, vb, zb, wb, bb, sb_ref, ob_ref, reverse=True)


def gla_scan(p, z, wup_pad, b_gate):
    nl = L // GLA_BLK
    ctx0 = TL // GLA_BLK

    def row_f(b, s):
        return jnp.where(s == 0, ctx0 + b, b * nl + s - 1)

    def row_b(b, s):
        return jnp.where(s == 0, ctx0 + b, b * nl + nl - s)

    def specs(row):
        return [
            pl.BlockSpec((GLA_BLK, HK), lambda b, s: (row(b, s), 0)),
            pl.BlockSpec((GLA_BLK, HK), lambda b, s: (row(b, s), 1)),
            pl.BlockSpec((GLA_BLK, HV), lambda b, s: (row(b, s), (2 * HK) // HV)),
            pl.BlockSpec((GLA_BLK, ZPAD), lambda b, s: (row(b, s), 0)),
        ]

    in_specs = specs(row_f) + specs(row_b) + [
        pl.BlockSpec((None, 3 * ZPAD, HK), lambda b, s: (0, 0, 0)),
        pl.BlockSpec((None, 3 * ZPAD, HK), lambda b, s: (1, 0, 0)),
        pl.BlockSpec((None, 1, HK), lambda b, s: (0, 0, 0)),
        pl.BlockSpec((None, 1, HK), lambda b, s: (1, 0, 0)),
    ]
    out_sds = jax.ShapeDtypeStruct((T, HV), BF16)
    return pl.pallas_call(
        _gla_kernel,
        grid=(B, nl + 1),
        in_specs=in_specs,
        out_specs=[pl.BlockSpec((GLA_BLK, HV), lambda b, s: (row_f(b, s), 0)),
                   pl.BlockSpec((GLA_BLK, HV), lambda b, s: (row_b(b, s), 0))],
        out_shape=[out_sds, out_sds],
        scratch_shapes=[pltpu.VMEM((H, DV, DK), F32), pltpu.VMEM((H, DV, DK), F32)],
        compiler_params=_params(_ARB2),
        name="gla_scan",
    )(p, p, p, z, p, p, p, z, wup_pad, wup_pad, b_gate, b_gate)


def gla_layer(x, mods, g_row, w_in, w_gate_up, b_gate, g_head, w_out, n_out_rows):
    a_ops, a_specs, prologue = _prenorm_ops(x, mods, 1, g_row, TM)
    w_main = w_in[:, :GLA_MAIN].astype(BF16)
    w_z = jnp.pad(w_in[:, GLA_MAIN:], ((0, 0), (0, ZPAD - 2 * RANK))).astype(BF16)
    plain = lambda accs: accs[0]
    tn_in = 1024
    tn = 512
    p = fused_matmul(a_ops, a_specs, prologue, D, [w_main],
                     [pl.BlockSpec((D, tn_in), lambda i, j: (0, j))], [], [], plain,
                     GLA_MAIN, BF16, TM, tn_in, T, "gla_in")
    z = fused_matmul(a_ops, a_specs, prologue, D, [w_z],
                     [pl.BlockSpec((D, ZPAD), lambda i, j: (0, j))], [], [], plain,
                     ZPAD, F32, TM, ZPAD, T, "gla_rank")
    wup_pad = jnp.zeros((2, ZPAD, HK), F32)
    wup_pad = wup_pad.at[0, :RANK].set(w_gate_up[0]).at[1, RANK:2 * RANK].set(w_gate_up[1])
    wup_hi = wup_pad.astype(BF16)
    wup_lo = (wup_pad - wup_hi.astype(F32)).astype(BF16)
    wup_cat = jnp.concatenate([wup_hi, wup_lo, wup_hi], axis=1)
    o_f, o_b = gla_scan(p, z, wup_cat, b_gate.reshape(2, 1, HK))

    def out_prologue(h_ref, of_ref, ob_ref, r_ref, gh_ref):
        def piece(rows):
            for h in range(H):
                cols = slice(h * DV, (h + 1) * DV)
                o = of_ref[rows, cols].astype(F32) + ob_ref[rows, cols].astype(F32)
                gate = _silu(r_ref[rows, cols].astype(F32))
                h_ref[rows, cols] = (_rms(o) * gh_ref[...] * gate).astype(BF16)

        _rowwise(h_ref.shape[0], piece)

    tmo = TM
    o_ops = [o_f, o_b, p, g_head.reshape(1, DV)]
    o_specs = [pl.BlockSpec((tmo, HV), lambda i, j: (i, 0)),
               pl.BlockSpec((tmo, HV), lambda i, j: (i, 0)),
               pl.BlockSpec((tmo, HV), lambda i, j: (i, (2 * HK + HV) // HV)),
               _row_spec(DV)]
    e_ops, e_specs, epilogue = _residual_ops(x, mods, 5, tmo, tn)
    return fused_matmul(o_ops, o_specs, out_prologue, HV, [w_out.astype(BF16)],
                        [pl.BlockSpec((HV, tn), lambda i, j: (0, j))], e_ops, e_specs, epilogue,
                        D, F32, tmo, tn, n_out_rows, "gla_out")


def _fnet_group_kernel(x_ref, sh_ref, sc_ref, g_ref, cs_ref, o_ref, h_ref):
    _prenorm_into(h_ref, x_ref, sh_ref, sc_ref, g_ref)
    cs = cs_ref[...]
    for g in range(FG):
        ab = jnp.dot(h_ref[:, g * GW:(g + 1) * GW], cs, preferred_element_type=F32).astype(BF16)
        o_ref[:, g * GW:(g + 1) * GW] = ab[:, :GW]
        o_ref[:, D + g * GW:D + (g + 1) * GW] = ab[:, GW:]


def _fnet_pos_kernel(c_ref, s_ref, a_ref, b_ref, o_ref, *, scale):
    acc = jnp.dot(c_ref[...], a_ref[...], preferred_element_type=F32)
    acc = acc - jnp.dot(s_ref[...], b_ref[...], preferred_element_type=F32)
    o_ref[...] = (acc * scale).astype(o_ref.dtype)


def _dft_angles(rows, cols, n):
    return ((rows[:, None] * cols[None, :]) % n).astype(F32) * (2.0 * math.pi / n)


def _dft_tables(n):
    idx = jnp.arange(n, dtype=jnp.int32)
    m = 64
    if n <= m * m // 16:
        ang = _dft_angles(idx, idx, n)
        return jnp.cos(ang), jnp.sin(ang)
    a = _dft_angles(idx, jnp.arange(n // m, dtype=jnp.int32), n // m)
    b = _dft_angles(idx, jnp.arange(m, dtype=jnp.int32), n)
    ca, sa = jnp.cos(a)[:, :, None], jnp.sin(a)[:, :, None]
    cb, sb = jnp.cos(b)[:, None, :], jnp.sin(b)[:, None, :]
    return (ca * cb - sa * sb).reshape(n, n), (sa * cb + ca * sb).reshape(n, n)


def _fnet_pos(ab, seq, row0, tm, tn):
    cos_t, sin_t = _dft_tables(seq)
    cos_t, sin_t = cos_t.astype(BF16), sin_t.astype(BF16)
    blk0 = row0 // seq
    n_m = seq // tm
    n_n = D // tn
    in_specs = [
        pl.BlockSpec((tm, seq), lambda b, n, m: (m, 0)),
        pl.BlockSpec((tm, seq), lambda b, n, m: (m, 0)),
        pl.BlockSpec((seq, tn), lambda b, n, m: (blk0 + b, n)),
        pl.BlockSpec((seq, tn), lambda b, n, m: (blk0 + b, n_n + n)),
    ]
    return pl.pallas_call(
        functools.partial(_fnet_pos_kernel, scale=1.0 / math.sqrt(seq * GW)),
        grid=(B, n_n, n_m),
        in_specs=in_specs,
        out_specs=pl.BlockSpec((tm, tn), lambda b, n, m: (b * n_m + m, n)),
        out_shape=jax.ShapeDtypeStruct((B * seq, D), BF16),
        compiler_params=_params(("arbitrary",) * 3),
        name="fnet_pos",
    )(cos_t, sin_t, ab, ab)


def fnet_layer(x, mods, g_row, w_out, b_out):
    cos_g, sin_g = _dft_tables(GW)
    cs = jnp.concatenate([cos_g, sin_g], axis=1).astype(BF16)
    tmg = 512
    ab = pl.pallas_call(
        _fnet_group_kernel,
        grid=(T // tmg, 1),
        in_specs=[pl.BlockSpec((tmg, D), lambda i, j: (i, 0)),
                  _mod_spec(3, tmg), _mod_spec(4, tmg), _row_spec(D),
                  pl.BlockSpec((GW, 2 * GW), lambda i, j: (0, 0))],
        out_specs=pl.BlockSpec((tmg, 2 * D), lambda i, j: (i, 0)),
        out_shape=jax.ShapeDtypeStruct((T, 2 * D), BF16),
        scratch_shapes=[pltpu.VMEM((tmg, D), BF16)],
        compiler_params=_params(_ARB2),
        name="fnet_group",
    )(x, mods, mods, g_row, cs)
    f_lat = _fnet_pos(ab, L, 0, 512, 512)
    f_ctx = _fnet_pos(ab, LC, TL, LC, 512)
    lat_tiles = TL // TM
    assert TC == TM

    def pick_stream(h_ref, lat_ref, ctx_ref):
        @pl.when(pl.program_id(0) < lat_tiles)
        def _():
            h_ref[...] = lat_ref[...]

        @pl.when(pl.program_id(0) >= lat_tiles)
        def _():
            h_ref[...] = ctx_ref[...]

    tn = 1024
    e_ops, e_specs, epilogue = _residual_ops(x, mods, 5, TM, tn, bias=b_out.reshape(1, D))
    return fused_matmul([f_lat, f_ctx],
                        [pl.BlockSpec((TM, D), lambda i, j: (jnp.minimum(i, lat_tiles - 1), 0)),
                         pl.BlockSpec((TM, D), lambda i, j: (0, 0))],
                        pick_stream, D,
                        [w_out.astype(BF16)], [pl.BlockSpec((D, tn), lambda i, j: (0, j))],
                        e_ops, e_specs, epilogue, D, F32, TM, tn, T, "fnet_out")


def _conv_taps(u_ref, w_ref, pad_ref, y_ref, seg):
    n_seg = CONV_BLK // seg
    stride = seg + 2 * CONV_PAD
    zeros = jnp.zeros((CONV_PAD, D), F32)
    for s in range(n_seg):
        base = s * stride
        pad_ref[base:base + CONV_PAD, :] = zeros
        pad_ref[base + CONV_PAD:base + CONV_PAD + seg, :] = u_ref[s * seg:(s + 1) * seg, :]
        pad_ref[base + CONV_PAD + seg:base + stride, :] = zeros
    cw = 256
    rb = 64
    off0 = CONV_PAD - CONV_W // 2
    win = rb + 2 * CONV_PAD
    sub = 8

    def chan_body(ci, carry):
        c0 = pl.multiple_of(ci * cw, cw)
        for s in range(n_seg):
            for r in range(seg // rb):
                start = s * stride + r * rb
                window = pad_ref[start:start + win, pl.ds(c0, cw)]
                acc = jnp.zeros((rb, cw), F32)
                for res in range(sub):
                    shifted = window if res == 0 else pltpu.roll(window, win - res, axis=0)
                    for k in range(CONV_W):
                        if (k + off0) % sub == res:
                            a = k + off0 - res
                            acc = acc + w_ref[k:k + 1, pl.ds(c0, cw)] * shifted[a:a + rb, :]
                y_ref[s * seg + r * rb:s * seg + (r + 1) * rb, pl.ds(c0, cw)] = acc
        return carry

    lax.fori_loop(0, D // cw, chan_body, 0)


def _conv_kernel(u_ref, w_ref, bdw_ref, lg_ref, lb_ref, o_ref, pad_ref, y_ref, *, n_latent_blocks):
    latent = pl.program_id(0) < n_latent_blocks

    @pl.when(latent)
    def _():
        _conv_taps(u_ref, w_ref, pad_ref, y_ref, GRID_W)

    @pl.when(jnp.logical_not(latent))
    def _():
        _conv_taps(u_ref, w_ref, pad_ref, y_ref, LC)

    def norm_piece(rows):
        y = y_ref[rows, :] + bdw_ref[...]
        mu = jnp.mean(y, axis=-1, keepdims=True)
        var = jnp.mean(jnp.square(y - mu), axis=-1, keepdims=True)
        yn = (y - mu) * lax.rsqrt(var + EPS) * lg_ref[...] + lb_ref[...]
        o_ref[rows, :] = _silu(yn).astype(o_ref.dtype)

    _rowwise(CONV_BLK, norm_piece)


def conv_dw(u, w_dw, b_dw, ln_g, ln_b):
    pad_rows = max(CONV_BLK // seg * (seg + 2 * CONV_PAD) for seg in (GRID_W, LC))
    vec = pl.BlockSpec((1, D), lambda i: (0, 0))
    return pl.pallas_call(
        functools.partial(_conv_kernel, n_latent_blocks=TL // CONV_BLK),
        grid=(T // CONV_BLK,),
        in_specs=[pl.BlockSpec((CONV_BLK, D), lambda i: (i, 0)),
                  pl.BlockSpec((CONV_W, D), lambda i: (0, 0)), vec, vec, vec],
        out_specs=pl.BlockSpec((CONV_BLK, D), lambda i: (i, 0)),
        out_shape=jax.ShapeDtypeStruct((T, D), BF16),
        scratch_shapes=[pltpu.VMEM((pad_rows, D), F32), pltpu.VMEM((CONV_BLK, D), F32)],
        compiler_params=_params(("arbitrary",)),
        name="conv_dw",
    )(u, w_dw, b_dw, ln_g, ln_b)


def conv_layer(x, mods, g_row, w_pw1, b_pw1, w_dw, b_dw, ln_g, ln_b, w_pw2, b_pw2):
    a_ops, a_specs, prologue = _prenorm_ops(x, mods, 1, g_row, TM)
    tn = 512
    n_j = D // tn
    w1 = w_pw1.astype(BF16)
    b1 = b_pw1.reshape(1, 2 * D)

    def glu(accs, ba_ref, bg_ref):
        return (accs[0] + ba_ref[...]) * _sigmoid(accs[1] + bg_ref[...])

    u = fused_matmul(a_ops, a_specs, prologue, D, [w1, w1],
                     [pl.BlockSpec((D, tn), lambda i, j: (0, j)),
                      pl.BlockSpec((D, tn), lambda i, j: (0, n_j + j))],
                     [b1, b1],
                     [pl.BlockSpec((1, tn), lambda i, j: (0, j)),
                      pl.BlockSpec((1, tn), lambda i, j: (0, n_j + j))],
                     glu, D, F32, TM, tn, T, "conv_pw1")
    y = conv_dw(u, w_dw, b_dw.reshape(1, D), ln_g.reshape(1, D), ln_b.reshape(1, D))
    tn = 1024
    e_ops, e_specs, epilogue = _residual_ops(x, mods, 5, TM, tn, bias=b_pw2.reshape(1, D))
    return fused_matmul([y], [pl.BlockSpec((TM, D), lambda i, j: (i, 0))], None, D,
                        [w_pw2.astype(BF16)], [pl.BlockSpec((D, tn), lambda i, j: (0, j))],
                        e_ops, e_specs, epilogue, D, F32, TM, tn, T, "conv_pw2")


def kernel(x, c, ctx, c_ctx, w_mod, b_mod, norm_g, ffn_w_gate, ffn_w_up, ffn_w_down, gla_w_in, gla_w_gate_up, gla_b_gate, gla_g_head, gla_w_out, fnet_w_out, fnet_b_out, cm_w_pw1, cm_b_pw1, cm_w_dw, cm_b_dw, cm_ln_g, cm_ln_b, cm_w_pw2, cm_b_pw2, final_g):
    cvec = jnp.concatenate([c, c_ctx[None], jnp.zeros((MOD_ROWS - B - 1, D), F32)], axis=0)
    mods_all = adaln_all(cvec, w_mod, b_mod).reshape(DEPTH, MOD_ROWS * N_MOD, 1, D)
    wg, wu = cast_bf16(ffn_w_gate, ffn_w_up)
    wd, = cast_bf16(ffn_w_down)
    all_tiles = T // TM
    lat_tiles = TL // TM
    xs = None
    for i in range(DEPTH):
        kind = i % N_MIXERS
        j = i // N_MIXERS
        last = i == DEPTH - 1
        mods = mods_all[i]
        g_rows = [norm_g[i, k].reshape(1, D) for k in range(3)]
        if i == 0:
            xs = ffn_half(x.reshape(TL, D), mods, 0, g_rows[0], wg, wu, wd, i, 0, all_tiles,
                          x_tail=ctx.reshape(TC, D))
        else:
            xs = ffn_half(xs, mods, 0, g_rows[0], wg, wu, wd, i, 0, all_tiles)
        n_out = lat_tiles if last else all_tiles
        if kind == 0:
            xs = gla_layer(xs, mods, g_rows[1], gla_w_in[j], gla_w_gate_up[j], gla_b_gate[j],
                           gla_g_head[j], gla_w_out[j], n_out * TM)
        elif kind == 1:
            xs = fnet_layer(xs, mods, g_rows[1], fnet_w_out[j], fnet_b_out[j])
        else:
            xs = conv_layer(xs, mods, g_rows[1], cm_w_pw1[j], cm_b_pw1[j], cm_w_dw[j], cm_b_dw[j],
                            cm_ln_g[j], cm_ln_b[j], cm_w_pw2[j], cm_b_pw2[j])
        xs = ffn_half(xs, mods, 2, g_rows[2], wg, wu, wd, i, 1, n_out,
                      final_g=final_g.reshape(1, D) if last else None)
    return xs.reshape(B, L, D)
```

```python
import functools
import math

import jax
import jax.numpy as jnp
from jax import lax
from jax.experimental import pallas as pl
from jax.experimental.pallas import tpu as pltpu

F32 = jnp.float32
BF16 = jnp.bfloat16

D = 2048
B = 4
L = 4096
LC = 256
DEPTH = 4
N_MIXERS = 3
N_MOD = 9
DFF = 5632
EPS = 1e-6
GRID_W = 64

TL = B * L
TC = B * LC
T = TL + TC

H = 4
DK = D // (2 * H)
DV = D // H
RANK = 16
TAU = 16.0
CHUNK = 64
HK = H * DK
HV = H * DV
GLA_MAIN = 2 * HK + 2 * HV
GLA_BLK = 256
ZPAD = 128

FG = 8
GW = D // FG
CONV_W = 31
CONV_PAD = 16
CONV_BLK = 256

MOD_ROWS = 8
TM = 1024
TF = 512
ROW_PIECE = 16
ROW_GROUP = 128
CAST_STEPS = 64
VMEM_LIMIT = 60 * 1024 * 1024

_ARB2 = ("arbitrary", "arbitrary")


def _params(sem, limit=VMEM_LIMIT):
    return pltpu.CompilerParams(dimension_semantics=sem, vmem_limit_bytes=limit)


def _mod_row(t, tm):
    return jnp.minimum(t // (L // tm), B)


def _mod_spec(slot, tm):
    return pl.BlockSpec((None, 1, D), lambda i, j: (_mod_row(i, tm) * N_MOD + slot, 0, 0))


def _row_spec(n):
    return pl.BlockSpec((1, n), lambda i, j: (0, 0))


def _rms(x):
    return x * lax.rsqrt(jnp.mean(x * x, axis=-1, keepdims=True) + EPS)


def _rowwise(n_rows, piece):
    def trip(t, carry):
        for p in range(ROW_GROUP // ROW_PIECE):
            piece(pl.ds(pl.multiple_of(t * ROW_GROUP + p * ROW_PIECE, ROW_PIECE), ROW_PIECE))
        return carry

    lax.fori_loop(0, n_rows // ROW_GROUP, trip, 0)


def _prenorm_into(h_ref, x_ref, sh_ref, sc_ref, g_ref, mod_ref, copy_ref=None):
    mod_ref[0] = jnp.broadcast_to(g_ref[...] * (1.0 + sc_ref[...]), (ROW_PIECE, D))
    mod_ref[1] = jnp.broadcast_to(sh_ref[...], (ROW_PIECE, D))

    def piece(rows):
        x = x_ref[rows, :]
        if copy_ref is not None:
            copy_ref[rows, :] = x
        h_ref[rows, :] = (_rms(x) * mod_ref[0] + mod_ref[1]).astype(BF16)

    _rowwise(h_ref.shape[0], piece)


def _mod_scratch():
    return pltpu.VMEM((2, ROW_PIECE, D), F32)


def _sigmoid(x):
    return 1.0 / (1.0 + jnp.exp(-x))


def _silu(x):
    return x * _sigmoid(x)


def _log_sigmoid(x):
    return jnp.minimum(x, 0.0) - jnp.log(1.0 + jnp.exp(-jnp.abs(x)))


def _adaln_kernel(c_ref, w_ref, b_ref, o_ref):
    s = _silu(c_ref[...]).astype(BF16)
    o_ref[...] = jnp.dot(s, w_ref[...].astype(BF16), preferred_element_type=F32) + b_ref[...]


def adaln_all(cvec, w_mod, b_mod):
    tn = 1024
    n = N_MOD * D
    return pl.pallas_call(
        _adaln_kernel,
        grid=(DEPTH, n // tn),
        in_specs=[
            pl.BlockSpec((MOD_ROWS, D), lambda i, j: (0, 0)),
            pl.BlockSpec((None, D, tn), lambda i, j: (i, 0, j)),
            pl.BlockSpec((None, 1, tn), lambda i, j: (i, 0, j)),
        ],
        out_specs=pl.BlockSpec((None, MOD_ROWS, tn), lambda i, j: (i, 0, j)),
        out_shape=jax.ShapeDtypeStruct((DEPTH, MOD_ROWS, n), F32),
        compiler_params=_params(_ARB2),
        name="adaln",
    )(cvec, w_mod, b_mod.reshape(DEPTH, 1, n))


def _cast_kernel(*refs):
    n = len(refs) // 2
    for src, dst in zip(refs[:n], refs[n:]):
        dst[...] = src[...].astype(dst.dtype)


def cast_bf16(*ws):
    shape = ws[0].shape
    cols = shape[-1]
    rows = math.prod(shape[:-1])
    spec = pl.BlockSpec((rows // CAST_STEPS, cols), lambda i: (i, 0))
    outs = pl.pallas_call(
        _cast_kernel,
        grid=(CAST_STEPS,),
        in_specs=[spec] * len(ws),
        out_specs=[spec] * len(ws),
        out_shape=[jax.ShapeDtypeStruct((rows, cols), BF16)] * len(ws),
        compiler_params=_params(("arbitrary",)),
        name="cast_bf16",
    )(*[w.reshape(rows, cols) for w in ws])
    return [o.reshape(shape) for o in outs]


def _ffn_kernel(*refs, n_i, n_j, n_head, final):
    head_hbm, tail_hbm, sh_ref, sc_ref, gt_ref, g_ref, wg_ref, wu_ref, wd_ref = refs[:9]
    if final:
        fg_ref, o_ref, xbuf, h_ref, mod_ref, sem = refs[9:]
    else:
        o_ref, xbuf, h_ref, mod_ref, sem = refs[9:]
    i = pl.program_id(0)
    j = pl.program_id(1)

    def x_copy(tile, act):
        @pl.when(tile < n_head)
        def _():
            act(pltpu.make_async_copy(head_hbm.at[pl.ds(tile * TM, TM), :], xbuf, sem))

        @pl.when(tile >= n_head)
        def _():
            act(pltpu.make_async_copy(tail_hbm.at[pl.ds((tile - n_head) * TM, TM), :], xbuf, sem))

    @pl.when((i == 0) & (j == 0))
    def _():
        x_copy(0, lambda c: c.start())

    @pl.when(j == 0)
    def _():
        x_copy(i, lambda c: c.wait())
        _prenorm_into(h_ref, xbuf, sh_ref, sc_ref, g_ref, mod_ref, copy_ref=o_ref)

    @pl.when((j == 1) & (i + 1 < n_i))
    def _():
        x_copy(i + 1, lambda c: c.start())

    h = h_ref[...]
    g = jnp.dot(h, wg_ref[...], preferred_element_type=F32)
    u = jnp.dot(h, wu_ref[...], preferred_element_type=F32)
    a = (_silu(g) * u).astype(BF16)
    o_ref[...] += (0.5 * gt_ref[...]) * jnp.dot(a, wd_ref[...], preferred_element_type=F32)

    if final:
        @pl.when(j == n_j - 1)
        def _():
            o_ref[...] = _rms(o_ref[...]) * fg_ref[...]


def ffn_half(x, mods, k, g_row, wg, wu, wd, layer, half, n_tiles, final_g=None, x_tail=None):
    n_j = DFF // TF
    assert n_j >= 2
    final = final_g is not None
    n_head = n_tiles if x_tail is None else x.shape[0] // TM
    if x_tail is None:
        x_tail = x
    in_specs = [
        pl.BlockSpec(memory_space=pl.ANY), pl.BlockSpec(memory_space=pl.ANY),
        _mod_spec(3 * k, TM), _mod_spec(3 * k + 1, TM), _mod_spec(3 * k + 2, TM),
        _row_spec(D),
        pl.BlockSpec((None, None, D, TF), lambda i, j: (layer, half, 0, j)),
        pl.BlockSpec((None, None, D, TF), lambda i, j: (layer, half, 0, j)),
        pl.BlockSpec((None, None, TF, D), lambda i, j: (layer, half, j, 0)),
    ]
    args = [x, x_tail, mods, mods, mods, g_row, wg, wu, wd]
    if final:
        in_specs.append(_row_spec(D))
        args.append(final_g)
    return pl.pallas_call(
        functools.partial(_ffn_kernel, n_i=n_tiles, n_j=n_j, n_head=n_head, final=final),
        grid=(n_tiles, n_j),
        in_specs=in_specs,
        out_specs=pl.BlockSpec((TM, D), lambda i, j: (i, 0)),
        out_shape=jax.ShapeDtypeStruct((n_tiles * TM, D), F32),
        scratch_shapes=[pltpu.VMEM((TM, D), F32), pltpu.VMEM((TM, D), BF16), _mod_scratch(),
                        pltpu.SemaphoreType.DMA(())],
        compiler_params=_params(_ARB2),
        name="ffn_half",
    )(*args)


def _mm_kernel(*refs, n_a, n_w, n_e, prologue, epilogue):
    a_refs = refs[:n_a]
    w_refs = refs[n_a:n_a + n_w]
    e_refs = refs[n_a + n_w:n_a + n_w + n_e]
    o_ref = refs[n_a + n_w + n_e]
    if prologue is None:
        h = a_refs[0][...]
    else:
        h_ref, *p_refs = refs[n_a + n_w + n_e + 1:]

        @pl.when(pl.program_id(1) == 0)
        def _():
            prologue(h_ref, *a_refs, *p_refs)

        h = h_ref[...]
    accs = [jnp.dot(h, w[...], preferred_element_type=F32) for w in w_refs]
    o_ref[...] = epilogue(accs, *e_refs).astype(o_ref.dtype)


def fused_matmul(a_ops, a_specs, prologue, k_dim, w_ops, w_specs, e_ops, e_specs, epilogue,
                 n_out, out_dtype, tm, tn, n_rows, name, prologue_scratch=()):
    n_j = n_out // tn
    scratch = [] if prologue is None else [pltpu.VMEM((tm, k_dim), BF16), *prologue_scratch]
    return pl.pallas_call(
        functools.partial(_mm_kernel, n_a=len(a_ops), n_w=len(w_ops), n_e=len(e_ops),
                          prologue=prologue, epilogue=epilogue),
        grid=(n_rows // tm, n_j),
        in_specs=list(a_specs) + list(w_specs) + list(e_specs),
        out_specs=pl.BlockSpec((tm, tn), lambda i, j: (i, j)),
        out_shape=jax.ShapeDtypeStruct((n_rows, n_out), out_dtype),
        scratch_shapes=scratch,
        compiler_params=_params(_ARB2),
        name=name,
    )(*a_ops, *w_ops, *e_ops)


def _prenorm_ops(x, mods, k, g_row, tm):
    ops = [x, mods, mods, g_row]
    specs = [pl.BlockSpec((tm, D), lambda i, j: (i, 0)),
             _mod_spec(3 * k, tm), _mod_spec(3 * k + 1, tm), _row_spec(D)]

    return ops, specs, _prenorm_into


def _residual_ops(x, mods, slot, tm, tn, bias=None):
    ops = [x, mods]
    specs = [pl.BlockSpec((tm, tn), lambda i, j: (i, j)),
             pl.BlockSpec((None, 1, tn), lambda i, j: (_mod_row(i, tm) * N_MOD + slot, 0, j))]
    if bias is None:
        def epilogue(accs, x_ref, gt_ref):
            return x_ref[...] + gt_ref[...] * accs[0]
    else:
        ops.append(bias)
        specs.append(pl.BlockSpec((1, tn), lambda i, j: (0, j)))

        def epilogue(accs, x_ref, gt_ref, b_ref):
            return x_ref[...] + gt_ref[...] * (accs[0] + b_ref[...])
    return ops, specs, epilogue


def _gla_dir(q_ref, k_ref, v_ref, z_ref, wup_ref, bg_ref, s_ref, o_ref, reverse):
    n = GLA_BLK
    z = z_ref[...]
    z_hi = z.astype(BF16)
    z_lo = (z - z_hi.astype(F32)).astype(BF16)
    x = jnp.dot(jnp.concatenate([z_hi, z_hi, z_lo], axis=1), wup_ref[...],
                preferred_element_type=F32) + bg_ref[...]
    g = _log_sigmoid(x) * (1.0 / TAU)
    g1 = g.astype(BF16)
    r1 = g - g1.astype(F32)
    g2 = r1.astype(BF16)
    g3 = (r1 - g2.astype(F32)).astype(BF16)
    row = lax.broadcasted_iota(jnp.int32, (n, n), 0)
    col = lax.broadcasted_iota(jnp.int32, (n, n), 1)
    shift = int(math.log2(CHUNK))
    rc = jnp.right_shift(row, shift)
    cc = jnp.right_shift(col, shift)
    causal = (col >= row) if reverse else (col <= row)
    dist = (cc - rc) if reverse else (rc - cc)
    tri = jnp.where((rc == cc) & causal, 1.0, 0.0).astype(BF16)
    b3 = jnp.dot(tri, jnp.concatenate([g1, g2, g3], axis=1), preferred_element_type=F32)
    bcum_all = (b3[:, :HK] + b3[:, HK:2 * HK]) + b3[:, 2 * HK:]
    for h in range(H):
        _gla_head(q_ref, k_ref, v_ref, s_ref, o_ref, bcum_all[:, h * DK:(h + 1) * DK], h,
                  causal, dist, reverse)


def _gla_head(q_ref, k_ref, v_ref, s_ref, o_ref, bcum, h, causal, dist, reverse):
    n = GLA_BLK
    nc = n // CHUNK
    nt = (((1,), (1,)), ((), ()))
    kcols = slice(h * DK, (h + 1) * DK)
    vcols = slice(h * DV, (h + 1) * DV)
    i_ref = CHUNK // 2 if reverse else CHUNK // 2 - 1
    i_last = 0 if reverse else CHUNK - 1
    b_last = [bcum[c * CHUNK + i_last:c * CHUNK + i_last + 1, :] for c in range(nc)]
    b_ref = [bcum[c * CHUNK + i_ref:c * CHUNK + i_ref + 1, :] for c in range(nc)]
    scan = list(range(nc - 1, -1, -1)) if reverse else list(range(nc))
    zero = jnp.zeros((1, DK), F32)
    pfx, sfx, mid2, mid3 = {}, {}, {}, {}
    run = zero
    for c in scan:
        pfx[c] = run
        run = run + b_last[c]
    total = run
    run = zero
    for c in reversed(scan):
        sfx[c] = run
        run = run + b_last[c]
    for p, c in enumerate(scan):
        mid2[c] = b_last[scan[p - 1]] if p >= 2 else zero
        mid3[c] = b_last[scan[p - 1]] + b_last[scan[p - 2]] if p >= 3 else zero

    def rows(vals):
        return jnp.concatenate([jnp.broadcast_to(vals[c], (CHUNK, DK)) for c in range(nc)], axis=0)

    def exp_rows(vals):
        return rows({c: jnp.exp(vals[c]) for c in range(nc)})

    qs = q_ref[:, kcols].astype(F32) * (DK ** -0.5)
    kf = k_ref[:, kcols].astype(F32)
    v = v_ref[:, vcols]
    d_in = bcum - rows(b_ref)
    q_in = (qs * jnp.exp(d_in)).astype(BF16)
    k_in = (kf * jnp.exp(-d_in)).astype(BF16)
    q_it = qs * jnp.exp(bcum)
    k_st = kf * jnp.exp(rows(b_last) - bcum)
    q_hat = (q_it * exp_rows(pfx)).astype(BF16)
    k_hat = (k_st * exp_rows(sfx)).astype(BF16)
    q_cat = jnp.concatenate([q_it.astype(BF16), (q_it * exp_rows(mid2)).astype(BF16),
                             (q_it * exp_rows(mid3)).astype(BF16)], axis=0)
    a_same = lax.dot_general(q_in, k_in, nt, preferred_element_type=F32)
    a_prev = lax.dot_general(q_cat, k_st.astype(BF16), nt, preferred_element_type=F32)
    att = jnp.where((dist == 0) & causal, a_same, 0.0)
    for dd in range(1, nc):
        att = jnp.where(dist == dd, a_prev[(dd - 1) * n:dd * n], att)
    st_old = s_ref[h]
    o = jnp.dot(att.astype(BF16), v, preferred_element_type=F32)
    o = o + lax.dot_general(q_hat, st_old.astype(BF16), nt, preferred_element_type=F32)
    o_ref[:, vcols] = o.astype(o_ref.dtype)
    s_ref[h] = jnp.exp(total) * st_old + lax.dot_general(
        v, k_hat, (((0,), (0,)), ((), ())), preferred_element_type=F32)


def _gla_kernel(qf, kf, vf, zf, qb, kb, vb, zb, wf, wb, bf, bb, of_ref, ob_ref, sf_ref, sb_ref):
    @pl.when(pl.program_id(1) == 0)
    def _():
        sf_ref[...] = jnp.zeros_like(sf_ref)
        sb_ref[...] = jnp.zeros_like(sb_ref)

    _gla_dir(qf, kf, vf, zf, wf, bf, sf_ref, of_ref, reverse=False)
    _gla_dir(qb, kb, vb, zb, wb, bb, sb_ref, ob_ref, reverse=True)


def gla_scan(p, z, wup_pad, b_gate):
    nl = L // GLA_BLK
    ctx0 = TL // GLA_BLK

    def row_f(b, s):
        return jnp.where(s == 0, ctx0 + b, b * nl + s - 1)

    def row_b(b, s):
        return jnp.where(s == 0, ctx0 + b, b * nl + nl - s)

    def specs(row):
        return [
            pl.BlockSpec((GLA_BLK, HK), lambda b, s: (row(b, s), 0)),
            pl.BlockSpec((GLA_BLK, HK), lambda b, s: (row(b, s), 1)),
            pl.BlockSpec((GLA_BLK, HV), lambda b, s: (row(b, s), (2 * HK) // HV)),
            pl.BlockSpec((GLA_BLK, ZPAD), lambda b, s: (row(b, s), 0)),
        ]

    in_specs = specs(row_f) + specs(row_b) + [
        pl.BlockSpec((None, 3 * ZPAD, HK), lambda b, s: (0, 0, 0)),
        pl.BlockSpec((None, 3 * ZPAD, HK), lambda b, s: (1, 0, 0)),
        pl.BlockSpec((None, 1, HK), lambda b, s: (0, 0, 0)),
        pl.BlockSpec((None, 1, HK), lambda b, s: (1, 0, 0)),
    ]
    out_sds = jax.ShapeDtypeStruct((T, HV), BF16)
    return pl.pallas_call(
        _gla_kernel,
        grid=(B, nl + 1),
        in_specs=in_specs,
        out_specs=[pl.BlockSpec((GLA_BLK, HV), lambda b, s: (row_f(b, s), 0)),
                   pl.BlockSpec((GLA_BLK, HV), lambda b, s: (row_b(b, s), 0))],
        out_shape=[out_sds, out_sds],
        scratch_shapes=[pltpu.VMEM((H, DV, DK), F32), pltpu.VMEM((H, DV, DK), F32)],
        compiler_params=_params(_ARB2),
        name="gla_scan",
    )(p, p, p, z, p, p, p, z, wup_pad, wup_pad, b_gate, b_gate)


def _gla_in_kernel(x_ref, sh_ref, sc_ref, g_ref, w_ref, wz_ref, p_ref, z_ref, h_ref, mod_ref):
    @pl.when(pl.program_id(1) == 0)
    def _():
        _prenorm_into(h_ref, x_ref, sh_ref, sc_ref, g_ref, mod_ref)
        z_ref[...] = jnp.dot(h_ref[...], wz_ref[...], preferred_element_type=F32)

    p_ref[...] = jnp.dot(h_ref[...], w_ref[...], preferred_element_type=F32).astype(p_ref.dtype)


def gla_layer(x, mods, g_row, w_in, w_gate_up, b_gate, g_head, w_out, n_out_rows):
    a_ops, a_specs, _ = _prenorm_ops(x, mods, 1, g_row, TM)
    w_main = w_in[:, :GLA_MAIN].astype(BF16)
    w_z = jnp.pad(w_in[:, GLA_MAIN:], ((0, 0), (0, ZPAD - 2 * RANK))).astype(BF16)
    tn_in = 1024
    tn = 512
    p, z = pl.pallas_call(
        _gla_in_kernel,
        grid=(T // TM, GLA_MAIN // tn_in),
        in_specs=list(a_specs) + [pl.BlockSpec((D, tn_in), lambda i, j: (0, j)),
                                  pl.BlockSpec((D, ZPAD), lambda i, j: (0, 0))],
        out_specs=[pl.BlockSpec((TM, tn_in), lambda i, j: (i, j)),
                   pl.BlockSpec((TM, ZPAD), lambda i, j: (i, 0))],
        out_shape=[jax.ShapeDtypeStruct((T, GLA_MAIN), BF16), jax.ShapeDtypeStruct((T, ZPAD), F32)],
        scratch_shapes=[pltpu.VMEM((TM, D), BF16), _mod_scratch()],
        compiler_params=_params(_ARB2),
        name="gla_in",
    )(*a_ops, w_main, w_z)
    wup_pad = jnp.zeros((2, ZPAD, HK), F32)
    wup_pad = wup_pad.at[0, :RANK].set(w_gate_up[0]).at[1, RANK:2 * RANK].set(w_gate_up[1])
    wup_hi = wup_pad.astype(BF16)
    wup_lo = (wup_pad - wup_hi.astype(F32)).astype(BF16)
    wup_cat = jnp.concatenate([wup_hi, wup_lo, wup_hi], axis=1)
    o_f, o_b = gla_scan(p, z, wup_cat, b_gate.reshape(2, 1, HK))

    def out_prologue(h_ref, of_ref, ob_ref, r_ref, gh_ref):
        def piece(rows):
            for h in range(H):
                cols = slice(h * DV, (h + 1) * DV)
                o = of_ref[rows, cols].astype(F32) + ob_ref[rows, cols].astype(F32)
                gate = _silu(r_ref[rows, cols].astype(F32))
                h_ref[rows, cols] = (_rms(o) * gh_ref[...] * gate).astype(BF16)

        _rowwise(h_ref.shape[0], piece)

    tmo = TM
    o_ops = [o_f, o_b, p, g_head.reshape(1, DV)]
    o_specs = [pl.BlockSpec((tmo, HV), lambda i, j: (i, 0)),
               pl.BlockSpec((tmo, HV), lambda i, j: (i, 0)),
               pl.BlockSpec((tmo, HV), lambda i, j: (i, (2 * HK + HV) // HV)),
               _row_spec(DV)]
    e_ops, e_specs, epilogue = _residual_ops(x, mods, 5, tmo, tn)
    return fused_matmul(o_ops, o_specs, out_prologue, HV, [w_out.astype(BF16)],
                        [pl.BlockSpec((HV, tn), lambda i, j: (0, j))], e_ops, e_specs, epilogue,
                        D, F32, tmo, tn, n_out_rows, "gla_out")


def _fnet_group_kernel(x_ref, sh_ref, sc_ref, g_ref, cs_ref, o_ref, h_ref, mod_ref):
    _prenorm_into(h_ref, x_ref, sh_ref, sc_ref, g_ref, mod_ref)
    cs = cs_ref[...]
    for g in range(FG):
        ab = jnp.dot(h_ref[:, g * GW:(g + 1) * GW], cs, preferred_element_type=F32).astype(BF16)
        o_ref[:, g * GW:(g + 1) * GW] = ab[:, :GW]
        o_ref[:, D + g * GW:D + (g + 1) * GW] = ab[:, GW:]


def _fnet_pos_kernel(c_ref, s_ref, a_ref, b_ref, o_ref, *, scale):
    acc = jnp.dot(c_ref[...], a_ref[...], preferred_element_type=F32)
    acc = acc - jnp.dot(s_ref[...], b_ref[...], preferred_element_type=F32)
    o_ref[...] = (acc * scale).astype(o_ref.dtype)


def _dft_angles(rows, cols, n):
    return ((rows[:, None] * cols[None, :]) % n).astype(F32) * (2.0 * math.pi / n)


def _dft_tables(n):
    idx = jnp.arange(n, dtype=jnp.int32)
    m = 64
    if n <= m * m // 16:
        ang = _dft_angles(idx, idx, n)
        return jnp.cos(ang), jnp.sin(ang)
    a = _dft_angles(idx, jnp.arange(n // m, dtype=jnp.int32), n // m)
    b = _dft_angles(idx, jnp.arange(m, dtype=jnp.int32), n)
    ca, sa = jnp.cos(a)[:, :, None], jnp.sin(a)[:, :, None]
    cb, sb = jnp.cos(b)[:, None, :], jnp.sin(b)[:, None, :]
    return (ca * cb - sa * sb).reshape(n, n), (sa * cb + ca * sb).reshape(n, n)


def _fnet_pos(ab, seq, row0, tm, tn):
    cos_t, sin_t = _dft_tables(seq)
    cos_t, sin_t = cos_t.astype(BF16), sin_t.astype(BF16)
    blk0 = row0 // seq
    n_m = seq // tm
    n_n = D // tn
    in_specs = [
        pl.BlockSpec((tm, seq), lambda b, n, m: (m, 0)),
        pl.BlockSpec((tm, seq), lambda b, n, m: (m, 0)),
        pl.BlockSpec((seq, tn), lambda b, n, m: (blk0 + b, n)),
        pl.BlockSpec((seq, tn), lambda b, n, m: (blk0 + b, n_n + n)),
    ]
    return pl.pallas_call(
        functools.partial(_fnet_pos_kernel, scale=1.0 / math.sqrt(seq * GW)),
        grid=(B, n_n, n_m),
        in_specs=in_specs,
        out_specs=pl.BlockSpec((tm, tn), lambda b, n, m: (b * n_m + m, n)),
        out_shape=jax.ShapeDtypeStruct((B * seq, D), BF16),
        compiler_params=_params(("arbitrary",) * 3),
        name="fnet_pos",
    )(cos_t, sin_t, ab, ab)


def fnet_layer(x, mods, g_row, w_out, b_out):
    cos_g, sin_g = _dft_tables(GW)
    cs = jnp.concatenate([cos_g, sin_g], axis=1).astype(BF16)
    tmg = 512
    ab = pl.pallas_call(
        _fnet_group_kernel,
        grid=(T // tmg, 1),
        in_specs=[pl.BlockSpec((tmg, D), lambda i, j: (i, 0)),
                  _mod_spec(3, tmg), _mod_spec(4, tmg), _row_spec(D),
                  pl.BlockSpec((GW, 2 * GW), lambda i, j: (0, 0))],
        out_specs=pl.BlockSpec((tmg, 2 * D), lambda i, j: (i, 0)),
        out_shape=jax.ShapeDtypeStruct((T, 2 * D), BF16),
        scratch_shapes=[pltpu.VMEM((tmg, D), BF16), _mod_scratch()],
        compiler_params=_params(_ARB2),
        name="fnet_group",
    )(x, mods, mods, g_row, cs)
    f_lat = _fnet_pos(ab, L, 0, 512, 512)
    f_ctx = _fnet_pos(ab, LC, TL, LC, 512)
    lat_tiles = TL // TM
    assert TC == TM

    def pick_stream(h_ref, lat_ref, ctx_ref):
        @pl.when(pl.program_id(0) < lat_tiles)
        def _():
            h_ref[...] = lat_ref[...]

        @pl.when(pl.program_id(0) >= lat_tiles)
        def _():
            h_ref[...] = ctx_ref[...]

    tn = 1024
    e_ops, e_specs, epilogue = _residual_ops(x, mods, 5, TM, tn, bias=b_out.reshape(1, D))
    return fused_matmul([f_lat, f_ctx],
                        [pl.BlockSpec((TM, D), lambda i, j: (jnp.minimum(i, lat_tiles - 1), 0)),
                         pl.BlockSpec((TM, D), lambda i, j: (0, 0))],
                        pick_stream, D,
                        [w_out.astype(BF16)], [pl.BlockSpec((D, tn), lambda i, j: (0, j))],
                        e_ops, e_specs, epilogue, D, F32, TM, tn, T, "fnet_out")


def _conv_taps(u_ref, w_ref, pad_ref, y_ref, seg):
    n_seg = CONV_BLK // seg
    stride = seg + 2 * CONV_PAD
    zeros = jnp.zeros((CONV_PAD, D), F32)
    for s in range(n_seg):
        base = s * stride
        pad_ref[base:base + CONV_PAD, :] = zeros
        pad_ref[base + CONV_PAD:base + CONV_PAD + seg, :] = u_ref[s * seg:(s + 1) * seg, :]
        pad_ref[base + CONV_PAD + seg:base + stride, :] = zeros
    cw = 256
    rb = 64
    off0 = CONV_PAD - CONV_W // 2
    win = rb + 2 * CONV_PAD
    sub = 8

    def chan_body(ci, carry):
        c0 = pl.multiple_of(ci * cw, cw)
        for s in range(n_seg):
            for r in range(seg // rb):
                start = s * stride + r * rb
                window = pad_ref[start:start + win, pl.ds(c0, cw)]
                acc = jnp.zeros((rb, cw), F32)
                for res in range(sub):
                    shifted = window if res == 0 else pltpu.roll(window, win - res, axis=0)
                    for k in range(CONV_W):
                        if (k + off0) % sub == res:
                            a = k + off0 - res
                            acc = acc + w_ref[k:k + 1, pl.ds(c0, cw)] * shifted[a:a + rb, :]
                y_ref[s * seg + r * rb:s * seg + (r + 1) * rb, pl.ds(c0, cw)] = acc
        return carry

    lax.fori_loop(0, D // cw, chan_body, 0)


def _conv_kernel(u_ref, w_ref, bdw_ref, lg_ref, lb_ref, o_ref, pad_ref, y_ref, *, n_latent_blocks):
    latent = pl.program_id(0) < n_latent_blocks

    @pl.when(latent)
    def _():
        _conv_taps(u_ref, w_ref, pad_ref, y_ref, GRID_W)

    @pl.when(jnp.logical_not(latent))
    def _():
        _conv_taps(u_ref, w_ref, pad_ref, y_ref, LC)

    def norm_piece(rows):
        y = y_ref[rows, :] + bdw_ref[...]
        mu = jnp.mean(y, axis=-1, keepdims=True)
        var = jnp.mean(jnp.square(y - mu), axis=-1, keepdims=True)
        yn = (y - mu) * lax.rsqrt(var + EPS) * lg_ref[...] + lb_ref[...]
        o_ref[rows, :] = _silu(yn).astype(o_ref.dtype)

    _rowwise(CONV_BLK, norm_piece)


def conv_dw(u, w_dw, b_dw, ln_g, ln_b):
    pad_rows = max(CONV_BLK // seg * (seg + 2 * CONV_PAD) for seg in (GRID_W, LC))
    vec = pl.BlockSpec((1, D), lambda i: (0, 0))
    return pl.pallas_call(
        functools.partial(_conv_kernel, n_latent_blocks=TL // CONV_BLK),
        grid=(T // CONV_BLK,),
        in_specs=[pl.BlockSpec((CONV_BLK, D), lambda i: (i, 0)),
                  pl.BlockSpec((CONV_W, D), lambda i: (0, 0)), vec, vec, vec],
        out_specs=pl.BlockSpec((CONV_BLK, D), lambda i: (i, 0)),
        out_shape=jax.ShapeDtypeStruct((T, D), BF16),
        scratch_shapes=[pltpu.VMEM((pad_rows, D), F32), pltpu.VMEM((CONV_BLK, D), F32)],
        compiler_params=_params(("arbitrary",)),
        name="conv_dw",
    )(u, w_dw, b_dw, ln_g, ln_b)


def conv_layer(x, mods, g_row, w_pw1, b_pw1, w_dw, b_dw, ln_g, ln_b, w_pw2, b_pw2):
    a_ops, a_specs, prologue = _prenorm_ops(x, mods, 1, g_row, TM)
    tn = 512
    n_j = D // tn
    w1 = w_pw1.astype(BF16)
    b1 = b_pw1.reshape(1, 2 * D)

    def glu(accs, ba_ref, bg_ref):
        return (accs[0] + ba_ref[...]) * _sigmoid(accs[1] + bg_ref[...])

    u = fused_matmul(a_ops, a_specs, prologue, D, [w1, w1],
                     [pl.BlockSpec((D, tn), lambda i, j: (0, j)),
                      pl.BlockSpec((D, tn), lambda i, j: (0, n_j + j))],
                     [b1, b1],
                     [pl.BlockSpec((1, tn), lambda i, j: (0, j)),
                      pl.BlockSpec((1, tn), lambda i, j: (0, n_j + j))],
                     glu, D, F32, TM, tn, T, "conv_pw1", prologue_scratch=[_mod_scratch()])
    y = conv_dw(u, w_dw, b_dw.reshape(1, D), ln_g.reshape(1, D), ln_b.reshape(1, D))
    tn = 1024
    e_ops, e_specs, epilogue = _residual_ops(x, mods, 5, TM, tn, bias=b_pw2.reshape(1, D))
    return fused_matmul([y], [pl.BlockSpec((TM, D), lambda i, j: (i, 0))], None, D,
                        [w_pw2.astype(BF16)], [pl.BlockSpec((D, tn), lambda i, j: (0, j))],
                        e_ops, e_specs, epilogue, D, F32, TM, tn, T, "conv_pw2")


def kernel(x, c, ctx, c_ctx, w_mod, b_mod, norm_g, ffn_w_gate, ffn_w_up, ffn_w_down, gla_w_in, gla_w_gate_up, gla_b_gate, gla_g_head, gla_w_out, fnet_w_out, fnet_b_out, cm_w_pw1, cm_b_pw1, cm_w_dw, cm_b_dw, cm_ln_g, cm_ln_b, cm_w_pw2, cm_b_pw2, final_g):
    cvec = jnp.concatenate([c, c_ctx[None], jnp.zeros((MOD_ROWS - B - 1, D), F32)], axis=0)
    mods_all = adaln_all(cvec, w_mod, b_mod).reshape(DEPTH, MOD_ROWS * N_MOD, 1, D)
    wg, wu = cast_bf16(ffn_w_gate, ffn_w_up)
    wd, = cast_bf16(ffn_w_down)
    all_tiles = T // TM
    lat_tiles = TL // TM
    xs = None
    for i in range(DEPTH):
        kind = i % N_MIXERS
        j = i // N_MIXERS
        last = i == DEPTH - 1
        mods = mods_all[i]
        g_rows = [norm_g[i, k].reshape(1, D) for k in range(3)]
        if i == 0:
            xs = ffn_half(x.reshape(TL, D), mods, 0, g_rows[0], wg, wu, wd, i, 0, all_tiles,
                          x_tail=ctx.reshape(TC, D))
        else:
            xs = ffn_half(xs, mods, 0, g_rows[0], wg, wu, wd, i, 0, all_tiles)
        n_out = lat_tiles if last else all_tiles
        if kind == 0:
            xs = gla_layer(xs, mods, g_rows[1], gla_w_in[j], gla_w_gate_up[j], gla_b_gate[j],
                           gla_g_head[j], gla_w_out[j], n_out * TM)
        elif kind == 1:
            xs = fnet_layer(xs, mods, g_rows[1], fnet_w_out[j], fnet_b_out[j])
        else:
            xs = conv_layer(xs, mods, g_rows[1], cm_w_pw1[j], cm_b_pw1[j], cm_w_dw[j], cm_b_dw[j],
                            cm_ln_g[j], cm_ln_b[j], cm_w_pw2[j], cm_b_pw2[j])
        xs = ffn_half(xs, mods, 2, g_rows[2], wg, wu, wd, i, 1, n_out,
                      final_g=final_g.reshape(1, D) if last else None)
    return xs.reshape(B, L, D)
```

```python
import functools
import math

import jax
import jax.numpy as jnp
from jax import lax
from jax.experimental import pallas as pl
from jax.experimental.pallas import tpu as pltpu

F32 = jnp.float32
BF16 = jnp.bfloat16

D = 2048
B = 4
L = 4096
LC = 256
DEPTH = 4
N_MIXERS = 3
N_MOD = 9
DFF = 5632
EPS = 1e-6
GRID_W = 64

TL = B * L
TC = B * LC
T = TL + TC

H = 4
DK = D // (2 * H)
DV = D // H
RANK = 16
TAU = 16.0
CHUNK = 64
HK = H * DK
HV = H * DV
GLA_MAIN = 2 * HK + 2 * HV
GLA_BLK = 256
ZPAD = 128

FG = 8
GW = D // FG
CONV_W = 31
CONV_PAD = 16
CONV_BLK = 256

MOD_ROWS = 8
TM = 1024
TF = 512
ROW_PIECE = 16
ROW_GROUP = 128
CAST_STEPS = 64
VMEM_LIMIT = 60 * 1024 * 1024

_ARB2 = ("arbitrary", "arbitrary")


def _params(sem, limit=VMEM_LIMIT):
    return pltpu.CompilerParams(dimension_semantics=sem, vmem_limit_bytes=limit)


def _mod_row(t, tm):
    return jnp.minimum(t // (L // tm), B)


def _mod_spec(slot, tm):
    return pl.BlockSpec((None, 1, D), lambda i, j: (_mod_row(i, tm) * N_MOD + slot, 0, 0))


def _row_spec(n):
    return pl.BlockSpec((1, n), lambda i, j: (0, 0))


def _rms(x):
    return x * lax.rsqrt(jnp.mean(x * x, axis=-1, keepdims=True) + EPS)


def _rowwise(n_rows, piece):
    def trip(t, carry):
        for p in range(ROW_GROUP // ROW_PIECE):
            piece(pl.ds(pl.multiple_of(t * ROW_GROUP + p * ROW_PIECE, ROW_PIECE), ROW_PIECE))
        return carry

    lax.fori_loop(0, n_rows // ROW_GROUP, trip, 0)


def _prenorm_into(h_ref, x_ref, sh_ref, sc_ref, g_ref, mod_ref, copy_ref=None):
    mod_ref[0] = jnp.broadcast_to(g_ref[...] * (1.0 + sc_ref[...]), (ROW_PIECE, D))
    mod_ref[1] = jnp.broadcast_to(sh_ref[...], (ROW_PIECE, D))

    def piece(rows):
        x = x_ref[rows, :]
        if copy_ref is not None:
            copy_ref[rows, :] = x
        h_ref[rows, :] = (_rms(x) * mod_ref[0] + mod_ref[1]).astype(BF16)

    _rowwise(h_ref.shape[0], piece)


def _mod_scratch():
    return pltpu.VMEM((2, ROW_PIECE, D), F32)


def _sigmoid(x):
    return 1.0 / (1.0 + jnp.exp(-x))


def _silu(x):
    return x * _sigmoid(x)


def _log_sigmoid(x):
    return jnp.minimum(x, 0.0) - jnp.log(1.0 + jnp.exp(-jnp.abs(x)))


def _adaln_kernel(c_ref, w_ref, b_ref, o_ref):
    s = _silu(c_ref[...]).astype(BF16)
    o_ref[...] = jnp.dot(s, w_ref[...].astype(BF16), preferred_element_type=F32) + b_ref[...]


def adaln_all(cvec, w_mod, b_mod):
    tn = 1024
    n = N_MOD * D
    return pl.pallas_call(
        _adaln_kernel,
        grid=(DEPTH, n // tn),
        in_specs=[
            pl.BlockSpec((MOD_ROWS, D), lambda i, j: (0, 0)),
            pl.BlockSpec((None, D, tn), lambda i, j: (i, 0, j)),
            pl.BlockSpec((None, 1, tn), lambda i, j: (i, 0, j)),
        ],
        out_specs=pl.BlockSpec((None, MOD_ROWS, tn), lambda i, j: (i, 0, j)),
        out_shape=jax.ShapeDtypeStruct((DEPTH, MOD_ROWS, n), F32),
        compiler_params=_params(_ARB2),
        name="adaln",
    )(cvec, w_mod, b_mod.reshape(DEPTH, 1, n))


def _cast_kernel(*refs):
    n = len(refs) // 2
    for src, dst in zip(refs[:n], refs[n:]):
        dst[...] = src[...].astype(dst.dtype)


def cast_bf16(*ws):
    shape = ws[0].shape
    cols = shape[-1]
    rows = math.prod(shape[:-1])
    spec = pl.BlockSpec((rows // CAST_STEPS, cols), lambda i: (i, 0))
    outs = pl.pallas_call(
        _cast_kernel,
        grid=(CAST_STEPS,),
        in_specs=[spec] * len(ws),
        out_specs=[spec] * len(ws),
        out_shape=[jax.ShapeDtypeStruct((rows, cols), BF16)] * len(ws),
        compiler_params=_params(("arbitrary",)),
        name="cast_bf16",
    )(*[w.reshape(rows, cols) for w in ws])
    return [o.reshape(shape) for o in outs]


def _cast_gate_up_kernel(wg_ref, wu_ref, o_ref):
    o_ref[:, :TF] = wg_ref[...].astype(o_ref.dtype)
    o_ref[:, TF:] = wu_ref[...].astype(o_ref.dtype)


def cast_gate_up(w_gate, w_up):
    n_mat = DEPTH * 2
    spec = pl.BlockSpec((D, TF), lambda m, j: (m, j))
    out = pl.pallas_call(
        _cast_gate_up_kernel,
        grid=(n_mat, DFF // TF),
        in_specs=[spec, spec],
        out_specs=pl.BlockSpec((D, 2 * TF), lambda m, j: (m, j)),
        out_shape=jax.ShapeDtypeStruct((n_mat * D, 2 * DFF), BF16),
        compiler_params=_params(_ARB2),
        name="cast_gate_up",
    )(w_gate.reshape(n_mat * D, DFF), w_up.reshape(n_mat * D, DFF))
    return out.reshape(DEPTH, 2, D, 2 * DFF)


def _ffn_kernel(*refs, n_i, n_j, n_head, final):
    head_hbm, tail_hbm, sh_ref, sc_ref, gt_ref, g_ref, wgu_ref, wd_ref = refs[:8]
    if final:
        fg_ref, o_ref, xbuf, h_ref, mod_ref, sem = refs[8:]
    else:
        o_ref, xbuf, h_ref, mod_ref, sem = refs[8:]
    i = pl.program_id(0)
    j = pl.program_id(1)

    def x_copy(tile, act):
        @pl.when(tile < n_head)
        def _():
            act(pltpu.make_async_copy(head_hbm.at[pl.ds(tile * TM, TM), :], xbuf, sem))

        @pl.when(tile >= n_head)
        def _():
            act(pltpu.make_async_copy(tail_hbm.at[pl.ds((tile - n_head) * TM, TM), :], xbuf, sem))

    @pl.when((i == 0) & (j == 0))
    def _():
        x_copy(0, lambda c: c.start())

    @pl.when(j == 0)
    def _():
        x_copy(i, lambda c: c.wait())
        _prenorm_into(h_ref, xbuf, sh_ref, sc_ref, g_ref, mod_ref, copy_ref=o_ref)

    @pl.when((j == 1) & (i + 1 < n_i))
    def _():
        x_copy(i + 1, lambda c: c.start())

    gu = jnp.dot(h_ref[...], wgu_ref[...], preferred_element_type=F32)
    a = (_silu(gu[:, :TF]) * gu[:, TF:]).astype(BF16)
    o_ref[...] += (0.5 * gt_ref[...]) * jnp.dot(a, wd_ref[...], preferred_element_type=F32)

    if final:
        @pl.when(j == n_j - 1)
        def _():
            o_ref[...] = _rms(o_ref[...]) * fg_ref[...]


def ffn_half(x, mods, k, g_row, wgu, wd, layer, half, n_tiles, final_g=None, x_tail=None):
    n_j = DFF // TF
    assert n_j >= 2
    final = final_g is not None
    n_head = n_tiles if x_tail is None else x.shape[0] // TM
    if x_tail is None:
        x_tail = x
    in_specs = [
        pl.BlockSpec(memory_space=pl.ANY), pl.BlockSpec(memory_space=pl.ANY),
        _mod_spec(3 * k, TM), _mod_spec(3 * k + 1, TM), _mod_spec(3 * k + 2, TM),
        _row_spec(D),
        pl.BlockSpec((None, None, D, 2 * TF), lambda i, j: (layer, half, 0, j)),
        pl.BlockSpec((None, None, TF, D), lambda i, j: (layer, half, j, 0)),
    ]
    args = [x, x_tail, mods, mods, mods, g_row, wgu, wd]
    if final:
        in_specs.append(_row_spec(D))
        args.append(final_g)
    return pl.pallas_call(
        functools.partial(_ffn_kernel, n_i=n_tiles, n_j=n_j, n_head=n_head, final=final),
        grid=(n_tiles, n_j),
        in_specs=in_specs,
        out_specs=pl.BlockSpec((TM, D), lambda i, j: (i, 0)),
        out_shape=jax.ShapeDtypeStruct((n_tiles * TM, D), F32),
        scratch_shapes=[pltpu.VMEM((TM, D), F32), pltpu.VMEM((TM, D), BF16), _mod_scratch(),
                        pltpu.SemaphoreType.DMA(())],
        compiler_params=_params(_ARB2),
        name="ffn_half",
    )(*args)


def _mm_kernel(*refs, n_a, n_w, n_e, prologue, epilogue):
    a_refs = refs[:n_a]
    w_refs = refs[n_a:n_a + n_w]
    e_refs = refs[n_a + n_w:n_a + n_w + n_e]
    o_ref = refs[n_a + n_w + n_e]
    if prologue is None:
        h = a_refs[0][...]
    else:
        h_ref, *p_refs = refs[n_a + n_w + n_e + 1:]

        @pl.when(pl.program_id(1) == 0)
        def _():
            prologue(h_ref, *a_refs, *p_refs)

        h = h_ref[...]
    accs = [jnp.dot(h, w[...], preferred_element_type=F32) for w in w_refs]
    o_ref[...] = epilogue(accs, *e_refs).astype(o_ref.dtype)


def fused_matmul(a_ops, a_specs, prologue, k_dim, w_ops, w_specs, e_ops, e_specs, epilogue,
                 n_out, out_dtype, tm, tn, n_rows, name, prologue_scratch=()):
    n_j = n_out // tn
    scratch = [] if prologue is None else [pltpu.VMEM((tm, k_dim), BF16), *prologue_scratch]
    return pl.pallas_call(
        functools.partial(_mm_kernel, n_a=len(a_ops), n_w=len(w_ops), n_e=len(e_ops),
                          prologue=prologue, epilogue=epilogue),
        grid=(n_rows // tm, n_j),
        in_specs=list(a_specs) + list(w_specs) + list(e_specs),
        out_specs=pl.BlockSpec((tm, tn), lambda i, j: (i, j)),
        out_shape=jax.ShapeDtypeStruct((n_rows, n_out), out_dtype),
        scratch_shapes=scratch,
        compiler_params=_params(_ARB2),
        name=name,
    )(*a_ops, *w_ops, *e_ops)


def _prenorm_ops(x, mods, k, g_row, tm):
    ops = [x, mods, mods, g_row]
    specs = [pl.BlockSpec((tm, D), lambda i, j: (i, 0)),
             _mod_spec(3 * k, tm), _mod_spec(3 * k + 1, tm), _row_spec(D)]

    return ops, specs, _prenorm_into


def _residual_ops(x, mods, slot, tm, tn, bias=None):
    ops = [x, mods]
    specs = [pl.BlockSpec((tm, tn), lambda i, j: (i, j)),
             pl.BlockSpec((None, 1, tn), lambda i, j: (_mod_row(i, tm) * N_MOD + slot, 0, j))]
    if bias is None:
        def epilogue(accs, x_ref, gt_ref):
            return x_ref[...] + gt_ref[...] * accs[0]
    else:
        ops.append(bias)
        specs.append(pl.BlockSpec((1, tn), lambda i, j: (0, j)))

        def epilogue(accs, x_ref, gt_ref, b_ref):
            return x_ref[...] + gt_ref[...] * (accs[0] + b_ref[...])
    return ops, specs, epilogue


def _gla_dir(q_ref, k_ref, v_ref, z_ref, wup_ref, bg_ref, s_ref, o_ref, reverse):
    n = GLA_BLK
    z = z_ref[...]
    z_hi = z.astype(BF16)
    z_lo = (z - z_hi.astype(F32)).astype(BF16)
    x = jnp.dot(jnp.concatenate([z_hi, z_hi, z_lo], axis=1), wup_ref[...],
                preferred_element_type=F32) + bg_ref[...]
    g = _log_sigmoid(x) * (1.0 / TAU)
    g1 = g.astype(BF16)
    r1 = g - g1.astype(F32)
    g2 = r1.astype(BF16)
    g3 = (r1 - g2.astype(F32)).astype(BF16)
    row = lax.broadcasted_iota(jnp.int32, (n, n), 0)
    col = lax.broadcasted_iota(jnp.int32, (n, n), 1)
    shift = int(math.log2(CHUNK))
    rc = jnp.right_shift(row, shift)
    cc = jnp.right_shift(col, shift)
    causal = (col >= row) if reverse else (col <= row)
    dist = (cc - rc) if reverse else (rc - cc)
    tri = jnp.where((rc == cc) & causal, 1.0, 0.0).astype(BF16)
    b3 = jnp.dot(tri, jnp.concatenate([g1, g2, g3], axis=1), preferred_element_type=F32)
    bcum_all = (b3[:, :HK] + b3[:, HK:2 * HK]) + b3[:, 2 * HK:]
    for h in range(H):
        _gla_head(q_ref, k_ref, v_ref, s_ref, o_ref, bcum_all[:, h * DK:(h + 1) * DK], h,
                  causal, dist, reverse)


def _gla_head(q_ref, k_ref, v_ref, s_ref, o_ref, bcum, h, causal, dist, reverse):
    n = GLA_BLK
    nc = n // CHUNK
    nt = (((1,), (1,)), ((), ()))
    kcols = slice(h * DK, (h + 1) * DK)
    vcols = slice(h * DV, (h + 1) * DV)
    i_ref = CHUNK // 2 if reverse else CHUNK // 2 - 1
    i_last = 0 if reverse else CHUNK - 1
    b_last = [bcum[c * CHUNK + i_last:c * CHUNK + i_last + 1, :] for c in range(nc)]
    b_ref = [bcum[c * CHUNK + i_ref:c * CHUNK + i_ref + 1, :] for c in range(nc)]
    scan = list(range(nc - 1, -1, -1)) if reverse else list(range(nc))
    zero = jnp.zeros((1, DK), F32)
    pfx, sfx, mid2, mid3 = {}, {}, {}, {}
    run = zero
    for c in scan:
        pfx[c] = run
        run = run + b_last[c]
    total = run
    run = zero
    for c in reversed(scan):
        sfx[c] = run
        run = run + b_last[c]
    for p, c in enumerate(scan):
        mid2[c] = b_last[scan[p - 1]] if p >= 2 else zero
        mid3[c] = b_last[scan[p - 1]] + b_last[scan[p - 2]] if p >= 3 else zero

    def rows(vals):
        return jnp.concatenate([jnp.broadcast_to(vals[c], (CHUNK, DK)) for c in range(nc)], axis=0)

    def exp_rows(vals):
        return rows({c: jnp.exp(vals[c]) for c in range(nc)})

    qs = q_ref[:, kcols].astype(F32) * (DK ** -0.5)
    kf = k_ref[:, kcols].astype(F32)
    v = v_ref[:, vcols]
    d_in = bcum - rows(b_ref)
    q_in = (qs * jnp.exp(d_in)).astype(BF16)
    k_in = (kf * jnp.exp(-d_in)).astype(BF16)
    q_it = qs * jnp.exp(bcum)
    k_st = kf * jnp.exp(rows(b_last) - bcum)
    q_hat = (q_it * exp_rows(pfx)).astype(BF16)
    k_hat = (k_st * exp_rows(sfx)).astype(BF16)
    q_cat = jnp.concatenate([q_it.astype(BF16), (q_it * exp_rows(mid2)).astype(BF16),
                             (q_it * exp_rows(mid3)).astype(BF16)], axis=0)
    a_same = lax.dot_general(q_in, k_in, nt, preferred_element_type=F32)
    a_prev = lax.dot_general(q_cat, k_st.astype(BF16), nt, preferred_element_type=F32)
    att = jnp.where((dist == 0) & causal, a_same, 0.0)
    for dd in range(1, nc):
        att = jnp.where(dist == dd, a_prev[(dd - 1) * n:dd * n], att)
    st_old = s_ref[h]
    o = jnp.dot(att.astype(BF16), v, preferred_element_type=F32)
    o = o + lax.dot_general(q_hat, st_old.astype(BF16), nt, preferred_element_type=F32)
    o_ref[:, vcols] = o.astype(o_ref.dtype)
    s_ref[h] = jnp.exp(total) * st_old + lax.dot_general(
        v, k_hat, (((0,), (0,)), ((), ())), preferred_element_type=F32)


def _gla_kernel(qf, kf, vf, zf, qb, kb, vb, zb, wf, wb, bf, bb, of_ref, ob_ref, sf_ref, sb_ref):
    @pl.when(pl.program_id(1) == 0)
    def _():
        sf_ref[...] = jnp.zeros_like(sf_ref)
        sb_ref[...] = jnp.zeros_like(sb_ref)

    _gla_dir(qf, kf, vf, zf, wf, bf, sf_ref, of_ref, reverse=False)
    _gla_dir(qb, kb, vb, zb, wb, bb, sb_ref, ob_ref, reverse=True)


def gla_scan(p, z, wup_pad, b_gate):
    nl = L // GLA_BLK
    ctx0 = TL // GLA_BLK

    def row_f(b, s):
        return jnp.where(s == 0, ctx0 + b, b * nl + s - 1)

    def row_b(b, s):
        return jnp.where(s == 0, ctx0 + b, b * nl + nl - s)

    def specs(row):
        return [
            pl.BlockSpec((GLA_BLK, HK), lambda b, s: (row(b, s), 0)),
            pl.BlockSpec((GLA_BLK, HK), lambda b, s: (row(b, s), 1)),
            pl.BlockSpec((GLA_BLK, HV), lambda b, s: (row(b, s), (2 * HK) // HV)),
            pl.BlockSpec((GLA_BLK, ZPAD), lambda b, s: (row(b, s), 0)),
        ]

    in_specs = specs(row_f) + specs(row_b) + [
        pl.BlockSpec((None, 3 * ZPAD, HK), lambda b, s: (0, 0, 0)),
        pl.BlockSpec((None, 3 * ZPAD, HK), lambda b, s: (1, 0, 0)),
        pl.BlockSpec((None, 1, HK), lambda b, s: (0, 0, 0)),
        pl.BlockSpec((None, 1, HK), lambda b, s: (1, 0, 0)),
    ]
    out_sds = jax.ShapeDtypeStruct((T, HV), BF16)
    return pl.pallas_call(
        _gla_kernel,
        grid=(B, nl + 1),
        in_specs=in_specs,
        out_specs=[pl.BlockSpec((GLA_BLK, HV), lambda b, s: (row_f(b, s), 0)),
                   pl.BlockSpec((GLA_BLK, HV), lambda b, s: (row_b(b, s), 0))],
        out_shape=[out_sds, out_sds],
        scratch_shapes=[pltpu.VMEM((H, DV, DK), F32), pltpu.VMEM((H, DV, DK), F32)],
        compiler_params=_params(_ARB2),
        name="gla_scan",
    )(p, p, p, z, p, p, p, z, wup_pad, wup_pad, b_gate, b_gate)


def _gla_in_kernel(x_ref, sh_ref, sc_ref, g_ref, w_ref, wz_ref, p_ref, z_ref, h_ref, mod_ref):
    @pl.when(pl.program_id(1) == 0)
    def _():
        _prenorm_into(h_ref, x_ref, sh_ref, sc_ref, g_ref, mod_ref)
        z_ref[...] = jnp.dot(h_ref[...], wz_ref[...], preferred_element_type=F32)

    p_ref[...] = jnp.dot(h_ref[...], w_ref[...], preferred_element_type=F32).astype(p_ref.dtype)


def gla_layer(x, mods, g_row, w_in, w_gate_up, b_gate, g_head, w_out, n_out_rows):
    a_ops, a_specs, _ = _prenorm_ops(x, mods, 1, g_row, TM)
    w_main = w_in[:, :GLA_MAIN].astype(BF16)
    w_z = jnp.pad(w_in[:, GLA_MAIN:], ((0, 0), (0, ZPAD - 2 * RANK))).astype(BF16)
    tn_in = 1024
    tn = 512
    p, z = pl.pallas_call(
        _gla_in_kernel,
        grid=(T // TM, GLA_MAIN // tn_in),
        in_specs=list(a_specs) + [pl.BlockSpec((D, tn_in), lambda i, j: (0, j)),
                                  pl.BlockSpec((D, ZPAD), lambda i, j: (0, 0))],
        out_specs=[pl.BlockSpec((TM, tn_in), lambda i, j: (i, j)),
                   pl.BlockSpec((TM, ZPAD), lambda i, j: (i, 0))],
        out_shape=[jax.ShapeDtypeStruct((T, GLA_MAIN), BF16), jax.ShapeDtypeStruct((T, ZPAD), F32)],
        scratch_shapes=[pltpu.VMEM((TM, D), BF16), _mod_scratch()],
        compiler_params=_params(_ARB2),
        name="gla_in",
    )(*a_ops, w_main, w_z)
    wup_pad = jnp.zeros((2, ZPAD, HK), F32)
    wup_pad = wup_pad.at[0, :RANK].set(w_gate_up[0]).at[1, RANK:2 * RANK].set(w_gate_up[1])
    wup_hi = wup_pad.astype(BF16)
    wup_lo = (wup_pad - wup_hi.astype(F32)).astype(BF16)
    wup_cat = jnp.concatenate([wup_hi, wup_lo, wup_hi], axis=1)
    o_f, o_b = gla_scan(p, z, wup_cat, b_gate.reshape(2, 1, HK))

    def out_prologue(h_ref, of_ref, ob_ref, r_ref, gh_ref):
        def piece(rows):
            for h in range(H):
                cols = slice(h * DV, (h + 1) * DV)
                o = of_ref[rows, cols].astype(F32) + ob_ref[rows, cols].astype(F32)
                gate = _silu(r_ref[rows, cols].astype(F32))
                h_ref[rows, cols] = (_rms(o) * gh_ref[...] * gate).astype(BF16)

        _rowwise(h_ref.shape[0], piece)

    tmo = TM
    tn = 1024
    o_ops = [o_f, o_b, p, g_head.reshape(1, DV)]
    o_specs = [pl.BlockSpec((tmo, HV), lambda i, j: (i, 0)),
               pl.BlockSpec((tmo, HV), lambda i, j: (i, 0)),
               pl.BlockSpec((tmo, HV), lambda i, j: (i, (2 * HK + HV) // HV)),
               _row_spec(DV)]
    e_ops, e_specs, epilogue = _residual_ops(x, mods, 5, tmo, tn)
    return fused_matmul(o_ops, o_specs, out_prologue, HV, [w_out.astype(BF16)],
                        [pl.BlockSpec((HV, tn), lambda i, j: (0, j))], e_ops, e_specs, epilogue,
                        D, F32, tmo, tn, n_out_rows, "gla_out")


def _fnet_group_kernel(x_ref, sh_ref, sc_ref, g_ref, cs_ref, o_ref, h_ref, mod_ref):
    _prenorm_into(h_ref, x_ref, sh_ref, sc_ref, g_ref, mod_ref)
    cs = cs_ref[...]
    for g in range(FG):
        ab = jnp.dot(h_ref[:, g * GW:(g + 1) * GW], cs, preferred_element_type=F32).astype(BF16)
        o_ref[:, g * GW:(g + 1) * GW] = ab[:, :GW]
        o_ref[:, D + g * GW:D + (g + 1) * GW] = ab[:, GW:]


def _fnet_pos_kernel(c_ref, s_ref, a_ref, b_ref, o_ref, *, scale):
    acc = jnp.dot(c_ref[...], a_ref[...], preferred_element_type=F32)
    acc = acc - jnp.dot(s_ref[...], b_ref[...], preferred_element_type=F32)
    o_ref[...] = (acc * scale).astype(o_ref.dtype)


def _dft_angles(rows, cols, n):
    return ((rows[:, None] * cols[None, :]) % n).astype(F32) * (2.0 * math.pi / n)


def _dft_tables(n):
    idx = jnp.arange(n, dtype=jnp.int32)
    m = 64
    if n <= m * m // 16:
        ang = _dft_angles(idx, idx, n)
        return jnp.cos(ang), jnp.sin(ang)
    a = _dft_angles(idx, jnp.arange(n // m, dtype=jnp.int32), n // m)
    b = _dft_angles(idx, jnp.arange(m, dtype=jnp.int32), n)
    ca, sa = jnp.cos(a)[:, :, None], jnp.sin(a)[:, :, None]
    cb, sb = jnp.cos(b)[:, None, :], jnp.sin(b)[:, None, :]
    return (ca * cb - sa * sb).reshape(n, n), (sa * cb + ca * sb).reshape(n, n)


def _fnet_pos(ab, seq, row0, tm, tn):
    cos_t, sin_t = _dft_tables(seq)
    cos_t, sin_t = cos_t.astype(BF16), sin_t.astype(BF16)
    blk0 = row0 // seq
    n_m = seq // tm
    n_n = D // tn
    in_specs = [
        pl.BlockSpec((tm, seq), lambda b, n, m: (m, 0)),
        pl.BlockSpec((tm, seq), lambda b, n, m: (m, 0)),
        pl.BlockSpec((seq, tn), lambda b, n, m: (blk0 + b, n)),
        pl.BlockSpec((seq, tn), lambda b, n, m: (blk0 + b, n_n + n)),
    ]
    return pl.pallas_call(
        functools.partial(_fnet_pos_kernel, scale=1.0 / math.sqrt(seq * GW)),
        grid=(B, n_n, n_m),
        in_specs=in_specs,
        out_specs=pl.BlockSpec((tm, tn), lambda b, n, m: (b * n_m + m, n)),
        out_shape=jax.ShapeDtypeStruct((B * seq, D), BF16),
        compiler_params=_params(("arbitrary",) * 3),
        name="fnet_pos",
    )(cos_t, sin_t, ab, ab)


def fnet_layer(x, mods, g_row, w_out, b_out):
    cos_g, sin_g = _dft_tables(GW)
    cs = jnp.concatenate([cos_g, sin_g], axis=1).astype(BF16)
    tmg = 512
    ab = pl.pallas_call(
        _fnet_group_kernel,
        grid=(T // tmg, 1),
        in_specs=[pl.BlockSpec((tmg, D), lambda i, j: (i, 0)),
                  _mod_spec(3, tmg), _mod_spec(4, tmg), _row_spec(D),
                  pl.BlockSpec((GW, 2 * GW), lambda i, j: (0, 0))],
        out_specs=pl.BlockSpec((tmg, 2 * D), lambda i, j: (i, 0)),
        out_shape=jax.ShapeDtypeStruct((T, 2 * D), BF16),
        scratch_shapes=[pltpu.VMEM((tmg, D), BF16), _mod_scratch()],
        compiler_params=_params(_ARB2),
        name="fnet_group",
    )(x, mods, mods, g_row, cs)
    f_lat = _fnet_pos(ab, L, 0, 1024, 512)
    f_ctx = _fnet_pos(ab, LC, TL, LC, 512)
    lat_tiles = TL // TM
    assert TC == TM

    def pick_stream(h_ref, lat_ref, ctx_ref):
        @pl.when(pl.program_id(0) < lat_tiles)
        def _():
            h_ref[...] = lat_ref[...]

        @pl.when(pl.program_id(0) >= lat_tiles)
        def _():
            h_ref[...] = ctx_ref[...]

    tn = 1024
    e_ops, e_specs, epilogue = _residual_ops(x, mods, 5, TM, tn, bias=b_out.reshape(1, D))
    return fused_matmul([f_lat, f_ctx],
                        [pl.BlockSpec((TM, D), lambda i, j: (jnp.minimum(i, lat_tiles - 1), 0)),
                         pl.BlockSpec((TM, D), lambda i, j: (0, 0))],
                        pick_stream, D,
                        [w_out.astype(BF16)], [pl.BlockSpec((D, tn), lambda i, j: (0, j))],
                        e_ops, e_specs, epilogue, D, F32, TM, tn, T, "fnet_out")


def _conv_taps(u_ref, w_ref, pad_ref, y_ref, seg):
    n_seg = CONV_BLK // seg
    stride = seg + 2 * CONV_PAD
    zeros = jnp.zeros((CONV_PAD, D), F32)
    for s in range(n_seg):
        base = s * stride
        pad_ref[base:base + CONV_PAD, :] = zeros
        pad_ref[base + CONV_PAD:base + CONV_PAD + seg, :] = u_ref[s * seg:(s + 1) * seg, :]
        pad_ref[base + CONV_PAD + seg:base + stride, :] = zeros
    cw = 256
    rb = 64
    off0 = CONV_PAD - CONV_W // 2
    win = rb + 2 * CONV_PAD
    sub = 8

    def chan_body(ci, carry):
        c0 = pl.multiple_of(ci * cw, cw)
        for s in range(n_seg):
            for r in range(seg // rb):
                start = s * stride + r * rb
                window = pad_ref[start:start + win, pl.ds(c0, cw)]
                acc = jnp.zeros((rb, cw), F32)
                for res in range(sub):
                    shifted = window if res == 0 else pltpu.roll(window, win - res, axis=0)
                    for k in range(CONV_W):
                        if (k + off0) % sub == res:
                            a = k + off0 - res
                            acc = acc + w_ref[k:k + 1, pl.ds(c0, cw)] * shifted[a:a + rb, :]
                y_ref[s * seg + r * rb:s * seg + (r + 1) * rb, pl.ds(c0, cw)] = acc
        return carry

    lax.fori_loop(0, D // cw, chan_body, 0)


def _conv_kernel(u_ref, w_ref, bdw_ref, lg_ref, lb_ref, o_ref, pad_ref, y_ref, *, n_latent_blocks):
    latent = pl.program_id(0) < n_latent_blocks

    @pl.when(latent)
    def _():
        _conv_taps(u_ref, w_ref, pad_ref, y_ref, GRID_W)

    @pl.when(jnp.logical_not(latent))
    def _():
        _conv_taps(u_ref, w_ref, pad_ref, y_ref, LC)

    def norm_piece(rows):
        y = y_ref[rows, :] + bdw_ref[...]
        mu = jnp.mean(y, axis=-1, keepdims=True)
        var = jnp.mean(jnp.square(y - mu), axis=-1, keepdims=True)
        yn = (y - mu) * lax.rsqrt(var + EPS) * lg_ref[...] + lb_ref[...]
        o_ref[rows, :] = _silu(yn).astype(o_ref.dtype)

    _rowwise(CONV_BLK, norm_piece)


def conv_dw(u, w_dw, b_dw, ln_g, ln_b):
    pad_rows = max(CONV_BLK // seg * (seg + 2 * CONV_PAD) for seg in (GRID_W, LC))
    vec = pl.BlockSpec((1, D), lambda i: (0, 0))
    return pl.pallas_call(
        functools.partial(_conv_kernel, n_latent_blocks=TL // CONV_BLK),
        grid=(T // CONV_BLK,),
        in_specs=[pl.BlockSpec((CONV_BLK, D), lambda i: (i, 0)),
                  pl.BlockSpec((CONV_W, D), lambda i: (0, 0)), vec, vec, vec],
        out_specs=pl.BlockSpec((CONV_BLK, D), lambda i: (i, 0)),
        out_shape=jax.ShapeDtypeStruct((T, D), BF16),
        scratch_shapes=[pltpu.VMEM((pad_rows, D), F32), pltpu.VMEM((CONV_BLK, D), F32)],
        compiler_params=_params(("arbitrary",)),
        name="conv_dw",
    )(u, w_dw, b_dw, ln_g, ln_b)


def conv_layer(x, mods, g_row, w_pw1, b_pw1, w_dw, b_dw, ln_g, ln_b, w_pw2, b_pw2):
    a_ops, a_specs, prologue = _prenorm_ops(x, mods, 1, g_row, TM)
    tn = 512
    n_j = D // tn
    w1 = w_pw1.astype(BF16)
    b1 = b_pw1.reshape(1, 2 * D)

    def glu(accs, ba_ref, bg_ref):
        return (accs[0] + ba_ref[...]) * _sigmoid(accs[1] + bg_ref[...])

    u = fused_matmul(a_ops, a_specs, prologue, D, [w1, w1],
                     [pl.BlockSpec((D, tn), lambda i, j: (0, j)),
                      pl.BlockSpec((D, tn), lambda i, j: (0, n_j + j))],
                     [b1, b1],
                     [pl.BlockSpec((1, tn), lambda i, j: (0, j)),
                      pl.BlockSpec((1, tn), lambda i, j: (0, n_j + j))],
                     glu, D, F32, TM, tn, T, "conv_pw1", prologue_scratch=[_mod_scratch()])
    y = conv_dw(u, w_dw, b_dw.reshape(1, D), ln_g.reshape(1, D), ln_b.reshape(1, D))
    tn = 1024
    e_ops, e_specs, epilogue = _residual_ops(x, mods, 5, TM, tn, bias=b_pw2.reshape(1, D))
    return fused_matmul([y], [pl.BlockSpec((TM, D), lambda i, j: (i, 0))], None, D,
                        [w_pw2.astype(BF16)], [pl.BlockSpec((D, tn), lambda i, j: (0, j))],
                        e_ops, e_specs, epilogue, D, F32, TM, tn, T, "conv_pw2")


def kernel(x, c, ctx, c_ctx, w_mod, b_mod, norm_g, ffn_w_gate, ffn_w_up, ffn_w_down, gla_w_in, gla_w_gate_up, gla_b_gate, gla_g_head, gla_w_out, fnet_w_out, fnet_b_out, cm_w_pw1, cm_b_pw1, cm_w_dw, cm_b_dw, cm_ln_g, cm_ln_b, cm_w_pw2, cm_b_pw2, final_g):
    cvec = jnp.concatenate([c, c_ctx[None], jnp.zeros((MOD_ROWS - B - 1, D), F32)], axis=0)
    mods_all = adaln_all(cvec, w_mod, b_mod).reshape(DEPTH, MOD_ROWS * N_MOD, 1, D)
    wgu = cast_gate_up(ffn_w_gate, ffn_w_up)
    wd, = cast_bf16(ffn_w_down)
    all_tiles = T // TM
    lat_tiles = TL // TM
    xs = None
    for i in range(DEPTH):
        kind = i % N_MIXERS
        j = i // N_MIXERS
        last = i == DEPTH - 1
        mods = mods_all[i]
        g_rows = [norm_g[i, k].reshape(1, D) for k in range(3)]
        if i == 0:
            xs = ffn_half(x.reshape(TL, D), mods, 0, g_rows[0], wgu, wd, i, 0, all_tiles,
                          x_tail=ctx.reshape(TC, D))
        else:
            xs = ffn_half(xs, mods, 0, g_rows[0], wgu, wd, i, 0, all_tiles)
        n_out = lat_tiles if last else all_tiles
        if kind == 0:
            xs = gla_layer(xs, mods, g_rows[1], gla_w_in[j], gla_w_gate_up[j], gla_b_gate[j],
                           gla_g_head[j], gla_w_out[j], n_out * TM)
        elif kind == 1:
            xs = fnet_layer(xs, mods, g_rows[1], fnet_w_out[j], fnet_b_out[j])
        else:
            xs = conv_layer(xs, mods, g_rows[1], cm_w_pw1[j], cm_b_pw1[j], cm_w_dw[j], cm_b_dw[j],
                            cm_ln_g[j], cm_ln_b[j], cm_w_pw2[j], cm_b_pw2[j])
        xs = ffn_half(xs, mods, 2, g_rows[2], wgu, wd, i, 1, n_out,
                      final_g=final_g.reshape(1, D) if last else None)
    return xs.reshape(B, L, D)
```

```python
import functools
import math

import jax
import jax.numpy as jnp
from jax import lax
from jax.experimental import pallas as pl
from jax.experimental.pallas import tpu as pltpu

F32 = jnp.float32
BF16 = jnp.bfloat16

D = 2048
B = 4
L = 4096
LC = 256
DEPTH = 4
N_MIXERS = 3
N_MOD = 9
DFF = 5632
EPS = 1e-6
GRID_W = 64

TL = B * L
TC = B * LC
T = TL + TC

H = 4
DK = D // (2 * H)
DV = D // H
RANK = 16
TAU = 16.0
CHUNK = 64
HK = H * DK
HV = H * DV
GLA_MAIN = 2 * HK + 2 * HV
GLA_BLK = 256
ZPAD = 128

FG = 8
GW = D // FG
CONV_W = 31
CONV_PAD = 16
CONV_BLK = 256

MOD_ROWS = 8
TM = 1024
TF = 512
ROW_PIECE = 16
ROW_GROUP = 128
CAST_STEPS = 64
VMEM_LIMIT = 60 * 1024 * 1024

_ARB2 = ("arbitrary", "arbitrary")


def _params(sem, limit=VMEM_LIMIT):
    return pltpu.CompilerParams(dimension_semantics=sem, vmem_limit_bytes=limit)


def _mod_row(t, tm):
    return jnp.minimum(t // (L // tm), B)


def _mod_spec(slot, tm):
    return pl.BlockSpec((None, 1, D), lambda i, j: (_mod_row(i, tm) * N_MOD + slot, 0, 0))


def _row_spec(n):
    return pl.BlockSpec((1, n), lambda i, j: (0, 0))


def _rms(x):
    return x * lax.rsqrt(jnp.mean(x * x, axis=-1, keepdims=True) + EPS)


def _rowwise(n_rows, piece):
    def trip(t, carry):
        for p in range(ROW_GROUP // ROW_PIECE):
            piece(pl.ds(pl.multiple_of(t * ROW_GROUP + p * ROW_PIECE, ROW_PIECE), ROW_PIECE))
        return carry

    lax.fori_loop(0, n_rows // ROW_GROUP, trip, 0)


def _prenorm_into(h_ref, x_ref, sh_ref, sc_ref, g_ref, mod_ref, copy_ref=None):
    mod_ref[0] = jnp.broadcast_to(g_ref[...] * (1.0 + sc_ref[...]), (ROW_PIECE, D))
    mod_ref[1] = jnp.broadcast_to(sh_ref[...], (ROW_PIECE, D))

    def piece(rows):
        x = x_ref[rows, :]
        if copy_ref is not None:
            copy_ref[rows, :] = x
        h_ref[rows, :] = (_rms(x) * mod_ref[0] + mod_ref[1]).astype(BF16)

    _rowwise(h_ref.shape[0], piece)


def _mod_scratch():
    return pltpu.VMEM((2, ROW_PIECE, D), F32)


def _sigmoid(x):
    return 1.0 / (1.0 + jnp.exp(-x))


def _silu(x):
    return x * _sigmoid(x)


def _log_sigmoid(x):
    return jnp.minimum(x, 0.0) - jnp.log(1.0 + jnp.exp(-jnp.abs(x)))


def _adaln_kernel(c_ref, w_ref, b_ref, o_ref):
    s = _silu(c_ref[...]).astype(BF16)
    o_ref[...] = jnp.dot(s, w_ref[...].astype(BF16), preferred_element_type=F32) + b_ref[...]


def adaln_all(cvec, w_mod, b_mod):
    tn = 1024
    n = N_MOD * D
    return pl.pallas_call(
        _adaln_kernel,
        grid=(DEPTH, n // tn),
        in_specs=[
            pl.BlockSpec((MOD_ROWS, D), lambda i, j: (0, 0)),
            pl.BlockSpec((None, D, tn), lambda i, j: (i, 0, j)),
            pl.BlockSpec((None, 1, tn), lambda i, j: (i, 0, j)),
        ],
        out_specs=pl.BlockSpec((None, MOD_ROWS, tn), lambda i, j: (i, 0, j)),
        out_shape=jax.ShapeDtypeStruct((DEPTH, MOD_ROWS, n), F32),
        compiler_params=_params(_ARB2),
        name="adaln",
    )(cvec, w_mod, b_mod.reshape(DEPTH, 1, n))


def _cast_kernel(*refs):
    n = len(refs) // 2
    for src, dst in zip(refs[:n], refs[n:]):
        dst[...] = src[...].astype(dst.dtype)


def cast_bf16(*ws):
    shape = ws[0].shape
    cols = shape[-1]
    rows = math.prod(shape[:-1])
    spec = pl.BlockSpec((rows // CAST_STEPS, cols), lambda i: (i, 0))
    outs = pl.pallas_call(
        _cast_kernel,
        grid=(CAST_STEPS,),
        in_specs=[spec] * len(ws),
        out_specs=[spec] * len(ws),
        out_shape=[jax.ShapeDtypeStruct((rows, cols), BF16)] * len(ws),
        compiler_params=_params(("arbitrary",)),
        name="cast_bf16",
    )(*[w.reshape(rows, cols) for w in ws])
    return [o.reshape(shape) for o in outs]


def _cast_gate_up_kernel(wg_ref, wu_ref, o_ref):
    o_ref[:, :TF] = wg_ref[...].astype(o_ref.dtype)
    o_ref[:, TF:] = wu_ref[...].astype(o_ref.dtype)


def cast_gate_up(w_gate, w_up):
    n_mat = DEPTH * 2
    spec = pl.BlockSpec((D, TF), lambda m, j: (m, j))
    out = pl.pallas_call(
        _cast_gate_up_kernel,
        grid=(n_mat, DFF // TF),
        in_specs=[spec, spec],
        out_specs=pl.BlockSpec((D, 2 * TF), lambda m, j: (m, j)),
        out_shape=jax.ShapeDtypeStruct((n_mat * D, 2 * DFF), BF16),
        compiler_params=_params(_ARB2),
        name="cast_gate_up",
    )(w_gate.reshape(n_mat * D, DFF), w_up.reshape(n_mat * D, DFF))
    return out.reshape(DEPTH, 2, D, 2 * DFF)


def _ffn_kernel(*refs, n_i, n_j, n_head, final):
    head_hbm, tail_hbm, sh_ref, sc_ref, gt_ref, g_ref, wgu_ref, wd_ref = refs[:8]
    if final:
        fg_ref, o_ref, xbuf, h_ref, mod_ref, sem = refs[8:]
    else:
        o_ref, xbuf, h_ref, mod_ref, sem = refs[8:]
    i = pl.program_id(0)
    j = pl.program_id(1)

    def x_copy(tile, act):
        @pl.when(tile < n_head)
        def _():
            act(pltpu.make_async_copy(head_hbm.at[pl.ds(tile * TM, TM), :], xbuf, sem))

        @pl.when(tile >= n_head)
        def _():
            act(pltpu.make_async_copy(tail_hbm.at[pl.ds((tile - n_head) * TM, TM), :], xbuf, sem))

    @pl.when((i == 0) & (j == 0))
    def _():
        x_copy(0, lambda c: c.start())

    @pl.when(j == 0)
    def _():
        x_copy(i, lambda c: c.wait())
        _prenorm_into(h_ref, xbuf, sh_ref, sc_ref, g_ref, mod_ref, copy_ref=o_ref)

    @pl.when((j == 1) & (i + 1 < n_i))
    def _():
        x_copy(i + 1, lambda c: c.start())

    gu = jnp.dot(h_ref[...], wgu_ref[...], preferred_element_type=F32)
    a = (_silu(gu[:, :TF]) * gu[:, TF:]).astype(BF16)
    o_ref[...] += (0.5 * gt_ref[...]) * jnp.dot(a, wd_ref[...], preferred_element_type=F32)

    if final:
        @pl.when(j == n_j - 1)
        def _():
            o_ref[...] = _rms(o_ref[...]) * fg_ref[...]


def ffn_half(x, mods, k, g_row, wgu, wd, layer, half, n_tiles, final_g=None, x_tail=None):
    n_j = DFF // TF
    assert n_j >= 2
    final = final_g is not None
    n_head = n_tiles if x_tail is None else x.shape[0] // TM
    if x_tail is None:
        x_tail = x
    in_specs = [
        pl.BlockSpec(memory_space=pl.ANY), pl.BlockSpec(memory_space=pl.ANY),
        _mod_spec(3 * k, TM), _mod_spec(3 * k + 1, TM), _mod_spec(3 * k + 2, TM),
        _row_spec(D),
        pl.BlockSpec((None, None, D, 2 * TF), lambda i, j: (layer, half, 0, j)),
        pl.BlockSpec((None, None, TF, D), lambda i, j: (layer, half, j, 0)),
    ]
    args = [x, x_tail, mods, mods, mods, g_row, wgu, wd]
    if final:
        in_specs.append(_row_spec(D))
        args.append(final_g)
    return pl.pallas_call(
        functools.partial(_ffn_kernel, n_i=n_tiles, n_j=n_j, n_head=n_head, final=final),
        grid=(n_tiles, n_j),
        in_specs=in_specs,
        out_specs=pl.BlockSpec((TM, D), lambda i, j: (i, 0)),
        out_shape=jax.ShapeDtypeStruct((n_tiles * TM, D), F32),
        scratch_shapes=[pltpu.VMEM((TM, D), F32), pltpu.VMEM((TM, D), BF16), _mod_scratch(),
                        pltpu.SemaphoreType.DMA(())],
        compiler_params=_params(_ARB2),
        name="ffn_half",
    )(*args)


def _mm_kernel(*refs, n_a, n_w, n_e, prologue, epilogue):
    a_refs = refs[:n_a]
    w_refs = refs[n_a:n_a + n_w]
    e_refs = refs[n_a + n_w:n_a + n_w + n_e]
    o_ref = refs[n_a + n_w + n_e]
    if prologue is None:
        h = a_refs[0][...]
    else:
        h_ref, *p_refs = refs[n_a + n_w + n_e + 1:]

        @pl.when(pl.program_id(1) == 0)
        def _():
            prologue(h_ref, *a_refs, *p_refs)

        h = h_ref[...]
    accs = [jnp.dot(h, w[...], preferred_element_type=F32) for w in w_refs]
    o_ref[...] = epilogue(accs, *e_refs).astype(o_ref.dtype)


def fused_matmul(a_ops, a_specs, prologue, k_dim, w_ops, w_specs, e_ops, e_specs, epilogue,
                 n_out, out_dtype, tm, tn, n_rows, name, prologue_scratch=()):
    n_j = n_out // tn
    scratch = [] if prologue is None else [pltpu.VMEM((tm, k_dim), BF16), *prologue_scratch]
    return pl.pallas_call(
        functools.partial(_mm_kernel, n_a=len(a_ops), n_w=len(w_ops), n_e=len(e_ops),
                          prologue=prologue, epilogue=epilogue),
        grid=(n_rows // tm, n_j),
        in_specs=list(a_specs) + list(w_specs) + list(e_specs),
        out_specs=pl.BlockSpec((tm, tn), lambda i, j: (i, j)),
        out_shape=jax.ShapeDtypeStruct((n_rows, n_out), out_dtype),
        scratch_shapes=scratch,
        compiler_params=_params(_ARB2),
        name=name,
    )(*a_ops, *w_ops, *e_ops)


def _prenorm_ops(x, mods, k, g_row, tm):
    ops = [x, mods, mods, g_row]
    specs = [pl.BlockSpec((tm, D), lambda i, j: (i, 0)),
             _mod_spec(3 * k, tm), _mod_spec(3 * k + 1, tm), _row_spec(D)]

    return ops, specs, _prenorm_into


def _residual_ops(x, mods, slot, tm, tn, bias=None):
    ops = [x, mods]
    specs = [pl.BlockSpec((tm, tn), lambda i, j: (i, j)),
             pl.BlockSpec((None, 1, tn), lambda i, j: (_mod_row(i, tm) * N_MOD + slot, 0, j))]
    if bias is None:
        def epilogue(accs, x_ref, gt_ref):
            return x_ref[...] + gt_ref[...] * accs[0]
    else:
        ops.append(bias)
        specs.append(pl.BlockSpec((1, tn), lambda i, j: (0, j)))

        def epilogue(accs, x_ref, gt_ref, b_ref):
            return x_ref[...] + gt_ref[...] * (accs[0] + b_ref[...])
    return ops, specs, epilogue


def _gla_dir(q_ref, k_ref, v_ref, z_ref, wup_ref, bg_ref, s_ref, o_ref, reverse):
    n = GLA_BLK
    z = z_ref[...]
    z_hi = z.astype(BF16)
    z_lo = (z - z_hi.astype(F32)).astype(BF16)
    x = jnp.dot(jnp.concatenate([z_hi, z_hi, z_lo], axis=1), wup_ref[...],
                preferred_element_type=F32) + bg_ref[...]
    g = _log_sigmoid(x) * (1.0 / TAU)
    g1 = g.astype(BF16)
    r1 = g - g1.astype(F32)
    g2 = r1.astype(BF16)
    g3 = (r1 - g2.astype(F32)).astype(BF16)
    row = lax.broadcasted_iota(jnp.int32, (n, n), 0)
    col = lax.broadcasted_iota(jnp.int32, (n, n), 1)
    shift = int(math.log2(CHUNK))
    rc = jnp.right_shift(row, shift)
    cc = jnp.right_shift(col, shift)
    causal = (col >= row) if reverse else (col <= row)
    dist = (cc - rc) if reverse else (rc - cc)
    tri = jnp.where((rc == cc) & causal, 1.0, 0.0).astype(BF16)
    b3 = jnp.dot(tri, jnp.concatenate([g1, g2, g3], axis=1), preferred_element_type=F32)
    bcum_all = (b3[:, :HK] + b3[:, HK:2 * HK]) + b3[:, 2 * HK:]
    for h in range(H):
        _gla_head(q_ref, k_ref, v_ref, s_ref, o_ref, bcum_all[:, h * DK:(h + 1) * DK], h,
                  causal, dist, reverse)


def _gla_head(q_ref, k_ref, v_ref, s_ref, o_ref, bcum, h, causal, dist, reverse):
    n = GLA_BLK
    nc = n // CHUNK
    nt = (((1,), (1,)), ((), ()))
    kcols = slice(h * DK, (h + 1) * DK)
    vcols = slice(h * DV, (h + 1) * DV)
    i_ref = CHUNK // 2 if reverse else CHUNK // 2 - 1
    i_last = 0 if reverse else CHUNK - 1
    b_last = [bcum[c * CHUNK + i_last:c * CHUNK + i_last + 1, :] for c in range(nc)]
    b_ref = [bcum[c * CHUNK + i_ref:c * CHUNK + i_ref + 1, :] for c in range(nc)]
    scan = list(range(nc - 1, -1, -1)) if reverse else list(range(nc))
    zero = jnp.zeros((1, DK), F32)
    pfx, sfx, mid2, mid3 = {}, {}, {}, {}
    run = zero
    for c in scan:
        pfx[c] = run
        run = run + b_last[c]
    total = run
    run = zero
    for c in reversed(scan):
        sfx[c] = run
        run = run + b_last[c]
    for p, c in enumerate(scan):
        mid2[c] = b_last[scan[p - 1]] if p >= 2 else zero
        mid3[c] = b_last[scan[p - 1]] + b_last[scan[p - 2]] if p >= 3 else zero

    def rows(vals):
        return jnp.concatenate([jnp.broadcast_to(vals[c], (CHUNK, DK)) for c in range(nc)], axis=0)

    def exp_rows(vals):
        return rows({c: jnp.exp(vals[c]) for c in range(nc)})

    qs = q_ref[:, kcols].astype(F32) * (DK ** -0.5)
    kf = k_ref[:, kcols].astype(F32)
    v = v_ref[:, vcols]
    d_in = bcum - rows(b_ref)
    q_in = (qs * jnp.exp(d_in)).astype(BF16)
    k_in = (kf * jnp.exp(-d_in)).astype(BF16)
    q_it = qs * jnp.exp(bcum)
    k_st = kf * jnp.exp(rows(b_last) - bcum)
    q_hat = (q_it * exp_rows(pfx)).astype(BF16)
    k_hat = (k_st * exp_rows(sfx)).astype(BF16)
    q_cat = jnp.concatenate([q_it.astype(BF16), (q_it * exp_rows(mid2)).astype(BF16),
                             (q_it * exp_rows(mid3)).astype(BF16)], axis=0)
    a_same = lax.dot_general(q_in, k_in, nt, preferred_element_type=F32)
    a_prev = lax.dot_general(q_cat, k_st.astype(BF16), nt, preferred_element_type=F32)
    att = jnp.where((dist == 0) & causal, a_same, 0.0)
    for dd in range(1, nc):
        att = jnp.where(dist == dd, a_prev[(dd - 1) * n:dd * n], att)
    st_old = s_ref[h]
    o = jnp.dot(att.astype(BF16), v, preferred_element_type=F32)
    o = o + lax.dot_general(q_hat, st_old.astype(BF16), nt, preferred_element_type=F32)
    o_ref[:, vcols] = o.astype(o_ref.dtype)
    s_ref[h] = jnp.exp(total) * st_old + lax.dot_general(
        v, k_hat, (((0,), (0,)), ((), ())), preferred_element_type=F32)


def _gla_kernel(qf, kf, vf, zf, qb, kb, vb, zb, wf, wb, bf, bb, of_ref, ob_ref, sf_ref, sb_ref):
    @pl.when(pl.program_id(1) == 0)
    def _():
        sf_ref[...] = jnp.zeros_like(sf_ref)
        sb_ref[...] = jnp.zeros_like(sb_ref)

    _gla_dir(qf, kf, vf, zf, wf, bf, sf_ref, of_ref, reverse=False)
    _gla_dir(qb, kb, vb, zb, wb, bb, sb_ref, ob_ref, reverse=True)


def gla_scan(p, z, wup_pad, b_gate):
    nl = L // GLA_BLK
    ctx0 = TL // GLA_BLK

    def row_f(b, s):
        return jnp.where(s == 0, ctx0 + b, b * nl + s - 1)

    def row_b(b, s):
        return jnp.where(s == 0, ctx0 + b, b * nl + nl - s)

    def specs(row):
        return [
            pl.BlockSpec((GLA_BLK, HK), lambda b, s: (row(b, s), 0)),
            pl.BlockSpec((GLA_BLK, HK), lambda b, s: (row(b, s), 1)),
            pl.BlockSpec((GLA_BLK, HV), lambda b, s: (row(b, s), (2 * HK) // HV)),
            pl.BlockSpec((GLA_BLK, ZPAD), lambda b, s: (row(b, s), 0)),
        ]

    in_specs = specs(row_f) + specs(row_b) + [
        pl.BlockSpec((None, 3 * ZPAD, HK), lambda b, s: (0, 0, 0)),
        pl.BlockSpec((None, 3 * ZPAD, HK), lambda b, s: (1, 0, 0)),
        pl.BlockSpec((None, 1, HK), lambda b, s: (0, 0, 0)),
        pl.BlockSpec((None, 1, HK), lambda b, s: (1, 0, 0)),
    ]
    out_sds = jax.ShapeDtypeStruct((T, HV), BF16)
    return pl.pallas_call(
        _gla_kernel,
        grid=(B, nl + 1),
        in_specs=in_specs,
        out_specs=[pl.BlockSpec((GLA_BLK, HV), lambda b, s: (row_f(b, s), 0)),
                   pl.BlockSpec((GLA_BLK, HV), lambda b, s: (row_b(b, s), 0))],
        out_shape=[out_sds, out_sds],
        scratch_shapes=[pltpu.VMEM((H, DV, DK), F32), pltpu.VMEM((H, DV, DK), F32)],
        compiler_params=_params(_ARB2),
        name="gla_scan",
    )(p, p, p, z, p, p, p, z, wup_pad, wup_pad, b_gate, b_gate)


def _gla_in_kernel(x_ref, sh_ref, sc_ref, g_ref, w_ref, wz_ref, p_ref, z_ref, h_ref, mod_ref):
    @pl.when(pl.program_id(1) == 0)
    def _():
        _prenorm_into(h_ref, x_ref, sh_ref, sc_ref, g_ref, mod_ref)
        z_ref[...] = jnp.dot(h_ref[...], wz_ref[...], preferred_element_type=F32)

    p_ref[...] = jnp.dot(h_ref[...], w_ref[...], preferred_element_type=F32).astype(p_ref.dtype)


def gla_layer(x, mods, g_row, w_in, w_gate_up, b_gate, g_head, w_out, n_out_rows):
    a_ops, a_specs, _ = _prenorm_ops(x, mods, 1, g_row, TM)
    w_main = w_in[:, :GLA_MAIN].astype(BF16)
    w_z = jnp.pad(w_in[:, GLA_MAIN:], ((0, 0), (0, ZPAD - 2 * RANK))).astype(BF16)
    tn_in = 2048
    tn = 512
    p, z = pl.pallas_call(
        _gla_in_kernel,
        grid=(T // TM, GLA_MAIN // tn_in),
        in_specs=list(a_specs) + [pl.BlockSpec((D, tn_in), lambda i, j: (0, j)),
                                  pl.BlockSpec((D, ZPAD), lambda i, j: (0, 0))],
        out_specs=[pl.BlockSpec((TM, tn_in), lambda i, j: (i, j)),
                   pl.BlockSpec((TM, ZPAD), lambda i, j: (i, 0))],
        out_shape=[jax.ShapeDtypeStruct((T, GLA_MAIN), BF16), jax.ShapeDtypeStruct((T, ZPAD), F32)],
        scratch_shapes=[pltpu.VMEM((TM, D), BF16), _mod_scratch()],
        compiler_params=_params(_ARB2),
        name="gla_in",
    )(*a_ops, w_main, w_z)
    wup_pad = jnp.zeros((2, ZPAD, HK), F32)
    wup_pad = wup_pad.at[0, :RANK].set(w_gate_up[0]).at[1, RANK:2 * RANK].set(w_gate_up[1])
    wup_hi = wup_pad.astype(BF16)
    wup_lo = (wup_pad - wup_hi.astype(F32)).astype(BF16)
    wup_cat = jnp.concatenate([wup_hi, wup_lo, wup_hi], axis=1)
    o_f, o_b = gla_scan(p, z, wup_cat, b_gate.reshape(2, 1, HK))

    def out_prologue(h_ref, of_ref, ob_ref, r_ref, gh_ref):
        def piece(rows):
            for h in range(H):
                cols = slice(h * DV, (h + 1) * DV)
                o = of_ref[rows, cols].astype(F32) + ob_ref[rows, cols].astype(F32)
                gate = _silu(r_ref[rows, cols].astype(F32))
                h_ref[rows, cols] = (_rms(o) * gh_ref[...] * gate).astype(BF16)

        _rowwise(h_ref.shape[0], piece)

    tmo = TM
    tn = 1024
    o_ops = [o_f, o_b, p, g_head.reshape(1, DV)]
    o_specs = [pl.BlockSpec((tmo, HV), lambda i, j: (i, 0)),
               pl.BlockSpec((tmo, HV), lambda i, j: (i, 0)),
               pl.BlockSpec((tmo, HV), lambda i, j: (i, (2 * HK + HV) // HV)),
               _row_spec(DV)]
    e_ops, e_specs, epilogue = _residual_ops(x, mods, 5, tmo, tn)
    return fused_matmul(o_ops, o_specs, out_prologue, HV, [w_out.astype(BF16)],
                        [pl.BlockSpec((HV, tn), lambda i, j: (0, j))], e_ops, e_specs, epilogue,
                        D, F32, tmo, tn, n_out_rows, "gla_out")


def _fnet_group_kernel(x_ref, sh_ref, sc_ref, g_ref, cs_ref, o_ref, h_ref, mod_ref):
    _prenorm_into(h_ref, x_ref, sh_ref, sc_ref, g_ref, mod_ref)
    cs = cs_ref[...]
    for g in range(FG):
        ab = jnp.dot(h_ref[:, g * GW:(g + 1) * GW], cs, preferred_element_type=F32).astype(BF16)
        o_ref[:, g * GW:(g + 1) * GW] = ab[:, :GW]
        o_ref[:, D + g * GW:D + (g + 1) * GW] = ab[:, GW:]


def _fnet_pos_kernel(c_ref, s_ref, a_ref, b_ref, o_ref, *, scale):
    acc = jnp.dot(c_ref[...], a_ref[...], preferred_element_type=F32)
    acc = acc - jnp.dot(s_ref[...], b_ref[...], preferred_element_type=F32)
    o_ref[...] = (acc * scale).astype(o_ref.dtype)


def _dft_angles(rows, cols, n):
    return ((rows[:, None] * cols[None, :]) % n).astype(F32) * (2.0 * math.pi / n)


def _dft_tables(n):
    idx = jnp.arange(n, dtype=jnp.int32)
    m = 64
    if n <= m * m // 16:
        ang = _dft_angles(idx, idx, n)
        return jnp.cos(ang), jnp.sin(ang)
    a = _dft_angles(idx, jnp.arange(n // m, dtype=jnp.int32), n // m)
    b = _dft_angles(idx, jnp.arange(m, dtype=jnp.int32), n)
    ca, sa = jnp.cos(a)[:, :, None], jnp.sin(a)[:, :, None]
    cb, sb = jnp.cos(b)[:, None, :], jnp.sin(b)[:, None, :]
    return (ca * cb - sa * sb).reshape(n, n), (sa * cb + ca * sb).reshape(n, n)


def _fnet_pos(ab, seq, row0, tm, tn):
    cos_t, sin_t = _dft_tables(seq)
    cos_t, sin_t = cos_t.astype(BF16), sin_t.astype(BF16)
    blk0 = row0 // seq
    n_m = seq // tm
    n_n = D // tn
    in_specs = [
        pl.BlockSpec((tm, seq), lambda b, n, m: (m, 0)),
        pl.BlockSpec((tm, seq), lambda b, n, m: (m, 0)),
        pl.BlockSpec((seq, tn), lambda b, n, m: (blk0 + b, n)),
        pl.BlockSpec((seq, tn), lambda b, n, m: (blk0 + b, n_n + n)),
    ]
    return pl.pallas_call(
        functools.partial(_fnet_pos_kernel, scale=1.0 / math.sqrt(seq * GW)),
        grid=(B, n_n, n_m),
        in_specs=in_specs,
        out_specs=pl.BlockSpec((tm, tn), lambda b, n, m: (b * n_m + m, n)),
        out_shape=jax.ShapeDtypeStruct((B * seq, D), BF16),
        compiler_params=_params(("arbitrary",) * 3),
        name="fnet_pos",
    )(cos_t, sin_t, ab, ab)


def fnet_layer(x, mods, g_row, w_out, b_out):
    cos_g, sin_g = _dft_tables(GW)
    cs = jnp.concatenate([cos_g, sin_g], axis=1).astype(BF16)
    tmg = 512
    ab = pl.pallas_call(
        _fnet_group_kernel,
        grid=(T // tmg, 1),
        in_specs=[pl.BlockSpec((tmg, D), lambda i, j: (i, 0)),
                  _mod_spec(3, tmg), _mod_spec(4, tmg), _row_spec(D),
                  pl.BlockSpec((GW, 2 * GW), lambda i, j: (0, 0))],
        out_specs=pl.BlockSpec((tmg, 2 * D), lambda i, j: (i, 0)),
        out_shape=jax.ShapeDtypeStruct((T, 2 * D), BF16),
        scratch_shapes=[pltpu.VMEM((tmg, D), BF16), _mod_scratch()],
        compiler_params=_params(_ARB2),
        name="fnet_group",
    )(x, mods, mods, g_row, cs)
    f_lat = _fnet_pos(ab, L, 0, 1024, 512)
    f_ctx = _fnet_pos(ab, LC, TL, LC, 512)
    lat_tiles = TL // TM
    assert TC == TM

    def pick_stream(h_ref, lat_ref, ctx_ref):
        @pl.when(pl.program_id(0) < lat_tiles)
        def _():
            h_ref[...] = lat_ref[...]

        @pl.when(pl.program_id(0) >= lat_tiles)
        def _():
            h_ref[...] = ctx_ref[...]

    tn = 1024
    e_ops, e_specs, epilogue = _residual_ops(x, mods, 5, TM, tn, bias=b_out.reshape(1, D))
    return fused_matmul([f_lat, f_ctx],
                        [pl.BlockSpec((TM, D), lambda i, j: (jnp.minimum(i, lat_tiles - 1), 0)),
                         pl.BlockSpec((TM, D), lambda i, j: (0, 0))],
                        pick_stream, D,
                        [w_out.astype(BF16)], [pl.BlockSpec((D, tn), lambda i, j: (0, j))],
                        e_ops, e_specs, epilogue, D, F32, TM, tn, T, "fnet_out")


def _conv_taps(u_ref, w_ref, pad_ref, y_ref, seg):
    n_seg = CONV_BLK // seg
    stride = seg + 2 * CONV_PAD
    zeros = jnp.zeros((CONV_PAD, D), F32)
    for s in range(n_seg):
        base = s * stride
        pad_ref[base:base + CONV_PAD, :] = zeros
        pad_ref[base + CONV_PAD:base + CONV_PAD + seg, :] = u_ref[s * seg:(s + 1) * seg, :]
        pad_ref[base + CONV_PAD + seg:base + stride, :] = zeros
    cw = 256
    rb = 64
    off0 = CONV_PAD - CONV_W // 2
    win = rb + 2 * CONV_PAD
    sub = 8

    def chan_body(ci, carry):
        c0 = pl.multiple_of(ci * cw, cw)
        for s in range(n_seg):
            for r in range(seg // rb):
                start = s * stride + r * rb
                window = pad_ref[start:start + win, pl.ds(c0, cw)]
                acc = jnp.zeros((rb, cw), F32)
                for res in range(sub):
                    shifted = window if res == 0 else pltpu.roll(window, win - res, axis=0)
                    for k in range(CONV_W):
                        if (k + off0) % sub == res:
                            a = k + off0 - res
                            acc = acc + w_ref[k:k + 1, pl.ds(c0, cw)] * shifted[a:a + rb, :]
                y_ref[s * seg + r * rb:s * seg + (r + 1) * rb, pl.ds(c0, cw)] = acc
        return carry

    lax.fori_loop(0, D // cw, chan_body, 0)


def _conv_kernel(u_ref, w_ref, bdw_ref, lg_ref, lb_ref, o_ref, pad_ref, y_ref, *, n_latent_blocks):
    latent = pl.program_id(0) < n_latent_blocks

    @pl.when(latent)
    def _():
        _conv_taps(u_ref, w_ref, pad_ref, y_ref, GRID_W)

    @pl.when(jnp.logical_not(latent))
    def _():
        _conv_taps(u_ref, w_ref, pad_ref, y_ref, LC)

    def norm_piece(rows):
        y = y_ref[rows, :] + bdw_ref[...]
        mu = jnp.mean(y, axis=-1, keepdims=True)
        var = jnp.mean(jnp.square(y - mu), axis=-1, keepdims=True)
        yn = (y - mu) * lax.rsqrt(var + EPS) * lg_ref[...] + lb_ref[...]
        o_ref[rows, :] = _silu(yn).astype(o_ref.dtype)

    _rowwise(CONV_BLK, norm_piece)


def conv_dw(u, w_dw, b_dw, ln_g, ln_b):
    pad_rows = max(CONV_BLK // seg * (seg + 2 * CONV_PAD) for seg in (GRID_W, LC))
    vec = pl.BlockSpec((1, D), lambda i: (0, 0))
    return pl.pallas_call(
        functools.partial(_conv_kernel, n_latent_blocks=TL // CONV_BLK),
        grid=(T // CONV_BLK,),
        in_specs=[pl.BlockSpec((CONV_BLK, D), lambda i: (i, 0)),
                  pl.BlockSpec((CONV_W, D), lambda i: (0, 0)), vec, vec, vec],
        out_specs=pl.BlockSpec((CONV_BLK, D), lambda i: (i, 0)),
        out_shape=jax.ShapeDtypeStruct((T, D), BF16),
        scratch_shapes=[pltpu.VMEM((pad_rows, D), F32), pltpu.VMEM((CONV_BLK, D), F32)],
        compiler_params=_params(("arbitrary",)),
        name="conv_dw",
    )(u, w_dw, b_dw, ln_g, ln_b)


def conv_layer(x, mods, g_row, w_pw1, b_pw1, w_dw, b_dw, ln_g, ln_b, w_pw2, b_pw2):
    a_ops, a_specs, prologue = _prenorm_ops(x, mods, 1, g_row, TM)
    tn = 1024
    n_j = D // tn
    w1 = w_pw1.astype(BF16)
    b1 = b_pw1.reshape(1, 2 * D)

    def glu(accs, ba_ref, bg_ref):
        return (accs[0] + ba_ref[...]) * _sigmoid(accs[1] + bg_ref[...])

    u = fused_matmul(a_ops, a_specs, prologue, D, [w1, w1],
                     [pl.BlockSpec((D, tn), lambda i, j: (0, j)),
                      pl.BlockSpec((D, tn), lambda i, j: (0, n_j + j))],
                     [b1, b1],
                     [pl.BlockSpec((1, tn), lambda i, j: (0, j)),
                      pl.BlockSpec((1, tn), lambda i, j: (0, n_j + j))],
                     glu, D, F32, TM, tn, T, "conv_pw1", prologue_scratch=[_mod_scratch()])
    y = conv_dw(u, w_dw, b_dw.reshape(1, D), ln_g.reshape(1, D), ln_b.reshape(1, D))
    tn = 1024
    e_ops, e_specs, epilogue = _residual_ops(x, mods, 5, TM, tn, bias=b_pw2.reshape(1, D))
    return fused_matmul([y], [pl.BlockSpec((TM, D), lambda i, j: (i, 0))], None, D,
                        [w_pw2.astype(BF16)], [pl.BlockSpec((D, tn), lambda i, j: (0, j))],
                        e_ops, e_specs, epilogue, D, F32, TM, tn, T, "conv_pw2")


def kernel(x, c, ctx, c_ctx, w_mod, b_mod, norm_g, ffn_w_gate, ffn_w_up, ffn_w_down, gla_w_in, gla_w_gate_up, gla_b_gate, gla_g_head, gla_w_out, fnet_w_out, fnet_b_out, cm_w_pw1, cm_b_pw1, cm_w_dw, cm_b_dw, cm_ln_g, cm_ln_b, cm_w_pw2, cm_b_pw2, final_g):
    cvec = jnp.concatenate([c, c_ctx[None], jnp.zeros((MOD_ROWS - B - 1, D), F32)], axis=0)
    mods_all = adaln_all(cvec, w_mod, b_mod).reshape(DEPTH, MOD_ROWS * N_MOD, 1, D)
    wgu = cast_gate_up(ffn_w_gate, ffn_w_up)
    wd, = cast_bf16(ffn_w_down)
    all_tiles = T // TM
    lat_tiles = TL // TM
    xs = None
    for i in range(DEPTH):
        kind = i % N_MIXERS
        j = i // N_MIXERS
        last = i == DEPTH - 1
        mods = mods_all[i]
        g_rows = [norm_g[i, k].reshape(1, D) for k in range(3)]
        if i == 0:
            xs = ffn_half(x.reshape(TL, D), mods, 0, g_rows[0], wgu, wd, i, 0, all_tiles,
                          x_tail=ctx.reshape(TC, D))
        else:
            xs = ffn_half(xs, mods, 0, g_rows[0], wgu, wd, i, 0, all_tiles)
        n_out = lat_tiles if last else all_tiles
        if kind == 0:
            xs = gla_layer(xs, mods, g_rows[1], gla_w_in[j], gla_w_gate_up[j], gla_b_gate[j],
                           gla_g_head[j], gla_w_out[j], n_out * TM)
        elif kind == 1:
            xs = fnet_layer(xs, mods, g_rows[1], fnet_w_out[j], fnet_b_out[j])
        else:
            xs = conv_layer(xs, mods, g_rows[1], cm_w_pw1[j], cm_b_pw1[j], cm_w_dw[j], cm_b_dw[j],
                            cm_ln_g[j], cm_ln_b[j], cm_w_pw2[j], cm_b_pw2[j])
        xs = ffn_half(xs, mods, 2, g_rows[2], wgu, wd, i, 1, n_out,
                      final_g=final_g.reshape(1, D) if last else None)
    return xs.reshape(B, L, D)
```
